```python
import math
import jax, jax.numpy as jnp
from jax import lax
import numpy as np

D_MODEL = 1024
BATCH = 8
SEQ = 2048
DEPTH = 2

BLOCK_Q = 128
DIFF_HEADS = 4
DIFF_QK_DIM = 32
DIFF_V_DIM = 64
FOX_HEADS = 6
FOX_HEAD_DIM = 64
MLA_HEADS = 6
MLA_Q_RANK = 256
MLA_KV_RANK = 128
MLA_NOPE_DIM = 64
MLA_ROPE_DIM = 32
MLA_V_DIM = 64
ROPE_THETA = 10000.0
REL_BUCKETS = 32
REL_MAX_DIST = 128
DIFF_WIDTH = DIFF_HEADS * DIFF_V_DIM
FOX_WIDTH = FOX_HEADS * FOX_HEAD_DIM
MLA_WIDTH = MLA_HEADS * MLA_V_DIM
MIX_WIDTH = DIFF_WIDTH + FOX_WIDTH + MLA_WIDTH
IN_SIZES = (
    DIFF_HEADS * 2 * DIFF_QK_DIM,
    DIFF_HEADS * 2 * DIFF_QK_DIM,
    DIFF_WIDTH,
    FOX_WIDTH,
    FOX_WIDTH,
    FOX_WIDTH,
    FOX_HEADS,
    MLA_Q_RANK,
    MLA_KV_RANK,
    MLA_ROPE_DIM,
)
IN_WIDTH = sum(IN_SIZES)
IN_OFFSETS = tuple(int(o) for o in np.cumsum(IN_SIZES)[:-1])
D_FF_DENSE = 2816
N_EXPERTS = 8
TOP_K = 2
D_FF_EXPERT = 3584
N_DENSE = (DEPTH + 1) // 2
N_MOE = DEPTH // 2
NORM_EPS = 1e-6

kernel_name = "hymba_style_diff_fox_mla_moe_trunk"


def _rmsnorm(x, g):
    xf = x.astype(jnp.float32)
    y = xf * lax.rsqrt(jnp.mean(xf * xf, axis=-1, keepdims=True) + NORM_EPS)
    return (y * g.astype(jnp.float32)).astype(x.dtype)


def _sweep(block_fn, seq):
    return jnp.concatenate([block_fn(i * BLOCK_Q, (i + 1) * BLOCK_Q) for i in range(seq // BLOCK_Q)], axis=1)


def _rel_dist(q0, q1):
    return (q0 + jnp.arange(q1 - q0))[:, None] - jnp.arange(q1)[None, :]


def _causal_softmax(logits, n):
    logits = jnp.where(n >= 0, logits, -jnp.inf)
    return jax.nn.softmax(logits, axis=-1)


def _t5_bucket(n):
    max_exact = REL_BUCKETS // 2
    n = jnp.maximum(n, 0)
    nf = jnp.maximum(n, 1).astype(jnp.float32)
    large = max_exact + (jnp.log(nf / max_exact) / math.log(REL_MAX_DIST / max_exact)
                         * (REL_BUCKETS - max_exact)).astype(jnp.int32)
    large = jnp.minimum(large, REL_BUCKETS - 1)
    return jnp.where(n < max_exact, n, large)


def _rope_tables(seq):
    half = MLA_ROPE_DIM // 2
    freqs = ROPE_THETA ** (-jnp.arange(half, dtype=jnp.float32) / half)
    ang = jnp.arange(seq, dtype=jnp.float32)[:, None] * freqs[None, :]
    return jnp.cos(ang), jnp.sin(ang)


def _rope(x, cos, sin):
    x1, x2 = jnp.split(x, 2, axis=-1)
    cs, sn = cos[:, None, :], sin[:, None, :]
    return jnp.concatenate([x1 * cs - x2 * sn, x1 * sn + x2 * cs], axis=-1).astype(x.dtype)


def _lambda_init(layer_idx):
    return 0.8 - 0.6 * math.exp(-0.3 * layer_idx)


def _diff_attention(q, k, v, diff_lambda, subln_g, rel_bias, lam_init):
    B, S = q.shape[0], q.shape[1]
    scale = DIFF_QK_DIM ** -0.5
    lam_p = diff_lambda.astype(jnp.float32)
    lam = jnp.exp(jnp.sum(lam_p[0] * lam_p[1])) - jnp.exp(jnp.sum(lam_p[2] * lam_p[3])) + lam_init

    def block(q0, q1):
        n = _rel_dist(q0, q1)
        logits = jnp.einsum('btmhd,bsmhd->bhmts',
                            jnp.swapaxes(q[:, q0:q1], 2, 3), jnp.swapaxes(k[:, :q1], 2, 3)).astype(jnp.float32) * scale
        bias = jnp.transpose(rel_bias[_t5_bucket(n)], (2, 0, 1)).astype(jnp.float32)
        p = _causal_softmax(logits + bias[None, :, None], n)
        w = p[:, :, 0] - lam * p[:, :, 1]
        return jnp.einsum('bhts,bshd->bthd', w.astype(v.dtype), v[:, :q1])

    o = _sweep(block, S)
    o = _rmsnorm(o, subln_g) * (1.0 - lam_init)
    return o.reshape(B, S, DIFF_WIDTH)


def _forgetting_attention(q, k, v, f_logit):
    B, S = q.shape[0], q.shape[1]
    scale = FOX_HEAD_DIM ** -0.5
    log_f = jax.nn.log_sigmoid(f_logit.astype(jnp.float32))
    F = jnp.transpose(jnp.cumsum(log_f, axis=1), (0, 2, 1))

    def block(q0, q1):
        n = _rel_dist(q0, q1)
        logits = jnp.einsum('bthd,bshd->bhts', q[:, q0:q1], k[:, :q1]).astype(jnp.float32) * scale
        decay = F[:, :, q0:q1, None] - F[:, :, None, :q1]
        p = _causal_softmax(logits + decay, n)
        return jnp.einsum('bhts,bshd->bthd', p.astype(v.dtype), v[:, :q1])

    return _sweep(block, S).reshape(B, S, FOX_WIDTH)


def _latent_attention(cq, ckv, kr, q_norm_g, kv_norm_g, w_uq, w_ukv, cos, sin):
    B, S = cq.shape[0], cq.shape[1]
    scale = (MLA_NOPE_DIM + MLA_ROPE_DIM) ** -0.5
    q = (_rmsnorm(cq, q_norm_g) @ w_uq).reshape(B, S, MLA_HEADS, MLA_NOPE_DIM + MLA_ROPE_DIM)
    q_nope, q_rope = q[..., :MLA_NOPE_DIM], _rope(q[..., MLA_NOPE_DIM:], cos, sin)
    kv = (_rmsnorm(ckv, kv_norm_g) @ w_ukv).reshape(B, S, MLA_HEADS, MLA_NOPE_DIM + MLA_V_DIM)
    k_nope, v = kv[..., :MLA_NOPE_DIM], kv[..., MLA_NOPE_DIM:]
    k_rope = _rope(kr[:, :, None, :], cos, sin)[:, :, 0]

    def block(q0, q1):
        n = _rel_dist(q0, q1)
        logits = (jnp.einsum('bthd,bshd->bhts', q_nope[:, q0:q1], k_nope[:, :q1])
                  + jnp.einsum('bthr,bsr->bhts', q_rope[:, q0:q1], k_rope[:, :q1])).astype(jnp.float32) * scale
        p = _causal_softmax(logits, n)
        return jnp.einsum('bhts,bshd->bthd', p.astype(v.dtype), v[:, :q1])

    return _sweep(block, S).reshape(B, S, MLA_WIDTH)


def _token_mixer(h, w_in, b_forget, diff_lambda, diff_subln, rel_bias, mla_q_norm, mla_kv_norm,
                 w_uq, w_ukv, w_out, cos, sin, lam_init):
    B, S, _ = h.shape
    proj = h @ w_in
    dq, dk, dv, fq, fk, fv, ff, cq, ckv, kr = jnp.split(proj, IN_OFFSETS, axis=-1)
    diff_out = _diff_attention(dq.reshape(B, S, DIFF_HEADS, 2, DIFF_QK_DIM),
                               dk.reshape(B, S, DIFF_HEADS, 2, DIFF_QK_DIM),
                               dv.reshape(B, S, DIFF_HEADS, DIFF_V_DIM),
                               diff_lambda, diff_subln, rel_bias, lam_init)
    fox_out = _forgetting_attention(fq.reshape(B, S, FOX_HEADS, FOX_HEAD_DIM),
                                    fk.reshape(B, S, FOX_HEADS, FOX_HEAD_DIM),
                                    fv.reshape(B, S, FOX_HEADS, FOX_HEAD_DIM),
                                    ff + b_forget)
    mla_out = _latent_attention(cq, ckv, kr, mla_q_norm, mla_kv_norm, w_uq, w_ukv, cos, sin)
    merged = jnp.concatenate([diff_out, fox_out, mla_out], axis=-1)
    return merged @ w_out


def _swiglu(h, w_gate, w_up, w_down):
    return (jax.nn.silu(h @ w_gate) * (h @ w_up)) @ w_down


def _moe(h, router_w, w_gate, w_up, w_down):
    B, S, D = h.shape
    t = h.reshape(B * S, D)
    logits = (t @ router_w).astype(jnp.float32)
    top_val, top_idx = lax.top_k(logits, TOP_K)
    top_w = jax.nn.softmax(top_val, axis=-1)
    combine = jnp.sum(jax.nn.one_hot(top_idx, N_EXPERTS, dtype=jnp.float32) * top_w[..., None], axis=1)
    y = jnp.zeros_like(t)
    for e in range(N_EXPERTS):
        y = y + combine[:, e:e + 1].astype(t.dtype) * _swiglu(t, w_gate[e], w_up[e], w_down[e])
    return y.reshape(B, S, D)


def setup_inputs(seed: int = 0) -> dict:
    key = jax.random.key(seed)
    ks = jax.random.split(key, 24)
    f32 = jnp.float32

    def nrm(k, shape, scale):
        return jax.random.normal(k, shape, f32) * scale

    def gain(k, shape):
        return 1.0 + 0.02 * jax.random.normal(k, shape, f32)

    D = D_MODEL
    return {
        "x": nrm(ks[0], (BATCH, SEQ, D), 1.0),
        "c": nrm(ks[1], (BATCH, D), 1.0),
        "w_ada": nrm(ks[2], (DEPTH, D, 6 * D), 0.5 * D ** -0.5),
        "b_ada": nrm(ks[3], (DEPTH, 6 * D), 0.02),
        "attn_norm": gain(ks[4], (DEPTH, D)),
        "ffn_norm": gain(ks[5], (DEPTH, D)),
        "w_in": nrm(ks[6], (DEPTH, D, IN_WIDTH), D ** -0.5),
        "b_forget": jax.random.uniform(ks[7], (DEPTH, FOX_HEADS), f32, 1.0, 4.0),
        "diff_lambda": nrm(ks[8], (DEPTH, 4, DIFF_QK_DIM), 0.1),
        "diff_subln": gain(ks[9], (DEPTH, DIFF_V_DIM)),
        "rel_bias": nrm(ks[10], (REL_BUCKETS, DIFF_HEADS), 0.5),
        "mla_q_norm": gain(ks[11], (DEPTH, MLA_Q_RANK)),
        "mla_kv_norm": gain(ks[12], (DEPTH, MLA_KV_RANK)),
        "w_uq": nrm(ks[13], (DEPTH, MLA_Q_RANK, MLA_HEADS * (MLA_NOPE_DIM + MLA_ROPE_DIM)), MLA_Q_RANK ** -0.5),
        "w_ukv": nrm(ks[14], (DEPTH, MLA_KV_RANK, MLA_HEADS * (MLA_NOPE_DIM + MLA_V_DIM)), MLA_KV_RANK ** -0.5),
        "w_out": nrm(ks[15], (DEPTH, MIX_WIDTH, D), MIX_WIDTH ** -0.5),
        "ffn_w_gate": nrm(ks[16], (N_DENSE, D, D_FF_DENSE), D ** -0.5),
        "ffn_w_up": nrm(ks[17], (N_DENSE, D, D_FF_DENSE), D ** -0.5),
        "ffn_w_down": nrm(ks[18], (N_DENSE, D_FF_DENSE, D), D_FF_DENSE ** -0.5),
        "router_w": nrm(ks[19], (N_MOE, D, N_EXPERTS), D ** -0.5),
        "moe_w_gate": nrm(ks[20], (N_MOE, N_EXPERTS, D, D_FF_EXPERT), D ** -0.5),
        "moe_w_up": nrm(ks[21], (N_MOE, N_EXPERTS, D, D_FF_EXPERT), D ** -0.5),
        "moe_w_down": nrm(ks[22], (N_MOE, N_EXPERTS, D_FF_EXPERT, D), D_FF_EXPERT ** -0.5),
        "final_norm": gain(ks[23], (D,)),
    }


def reference(x, c, w_ada, b_ada, attn_norm, ffn_norm, w_in, b_forget, diff_lambda, diff_subln,
              rel_bias, mla_q_norm, mla_kv_norm, w_uq, w_ukv, w_out, ffn_w_gate, ffn_w_up,
              ffn_w_down, router_w, moe_w_gate, moe_w_up, moe_w_down, final_norm):
    S = x.shape[1]
    cos, sin = _rope_tables(S)
    c_act = jax.nn.silu(c)
    for l in range(DEPTH):
        mod = c_act @ w_ada[l] + b_ada[l]
        sh_a, sc_a, g_a, sh_f, sc_f, g_f = [m[:, None, :] for m in jnp.split(mod, 6, axis=-1)]
        h = _rmsnorm(x, attn_norm[l]) * (1.0 + sc_a) + sh_a
        mix = _token_mixer(h, w_in[l], b_forget[l], diff_lambda[l], diff_subln[l], rel_bias,
                           mla_q_norm[l], mla_kv_norm[l], w_uq[l], w_ukv[l], w_out[l],
                           cos, sin, _lambda_init(l))
        x = x + g_a * mix
        h = _rmsnorm(x, ffn_norm[l]) * (1.0 + sc_f) + sh_f
        i = l // 2
        if l % 2 == 0:
            y = _swiglu(h, ffn_w_gate[i], ffn_w_up[i], ffn_w_down[i])
        else:
            y = _moe(h, router_w[i], moe_w_gate[i], moe_w_up[i], moe_w_down[i])
        x = x + g_f * y
    return _rmsnorm(x, final_norm)
```

```python
import functools
import math

import numpy as np
import jax
import jax.numpy as jnp
from jax import lax
from jax.experimental import pallas as pl
from jax.experimental.pallas import tpu as pltpu

F32 = jnp.float32
BF16 = jnp.bfloat16

DIFF_HEADS = 4
DIFF_QK_DIM = 32
DIFF_V_DIM = 64
FOX_HEADS = 6
FOX_HEAD_DIM = 64
MLA_HEADS = 6
MLA_Q_RANK = 256
MLA_KV_RANK = 128
MLA_NOPE_DIM = 64
MLA_ROPE_DIM = 32
MLA_V_DIM = 64
ROPE_THETA = 10000.0
REL_BUCKETS = 32
REL_MAX_DIST = 128
DIFF_WIDTH = DIFF_HEADS * DIFF_V_DIM
FOX_WIDTH = FOX_HEADS * FOX_HEAD_DIM
MLA_WIDTH = MLA_HEADS * MLA_V_DIM
N_EXPERTS = 8
NORM_EPS = 1e-6

LANES = 128
VMEM_LIMIT_BYTES = 56 * 1024 * 1024

OFF_D = 0
OFF_F = OFF_D + 3 * DIFF_WIDTH
OFF_CQ = OFF_F + 3 * FOX_WIDTH
OFF_CKV = OFF_CQ + MLA_Q_RANK
OFF_KR = OFF_CKV + MLA_KV_RANK
OFF_KRS = OFF_KR + LANES
OFF_MISC = OFF_KRS + LANES
IN_PAD = OFF_MISC + LANES
MLA_QK_PAD = MLA_HEADS * LANES

ATTN_TQ = 256


def _cparams(sem):
    return pltpu.CompilerParams(dimension_semantics=sem, vmem_limit_bytes=VMEM_LIMIT_BYTES)


def _sigmoid(x):
    return 1.0 / (1.0 + jnp.exp(-x))


def _silu(x):
    return x * _sigmoid(x)


def _bucket_thresholds():
    max_exact = REL_BUCKETS // 2
    n = np.arange(0, 4096)
    nf = np.maximum(n, 1).astype(np.float32)
    large = max_exact + (np.log(nf / np.float32(max_exact)) / np.float32(math.log(REL_MAX_DIST / max_exact))
                         * np.float32(REL_BUCKETS - max_exact)).astype(np.int32)
    large = np.minimum(large, REL_BUCKETS - 1)
    bucket = np.where(n < max_exact, n, large)
    assert np.all(np.diff(bucket) >= 0)
    thr = [int(np.argmax(bucket >= k)) for k in range(REL_BUCKETS)]
    assert all(bucket[thr[k]] >= k for k in range(REL_BUCKETS))
    return thr, bucket


def _mod_kernel(c_ref, w_ref, b_ref, o_ref):
    ca = _silu(c_ref[...]).astype(BF16)
    w = w_ref[0].astype(BF16)
    o_ref[0] = jnp.dot(ca, w, preferred_element_type=F32) + b_ref[0]


def _modulation(c, w_ada, b_ada):
    depth, d, n6 = w_ada.shape
    b = c.shape[0]
    tn = n6 // 4
    return pl.pallas_call(
        _mod_kernel,
        grid=(depth, n6 // tn),
        in_specs=[
            pl.BlockSpec((b, d), lambda l, j: (0, 0)),
            pl.BlockSpec((1, d, tn), lambda l, j: (l, 0, j)),
            pl.BlockSpec((1, 1, tn), lambda l, j: (l, 0, j)),
        ],
        out_specs=pl.BlockSpec((1, b, tn), lambda l, j: (l, 0, j)),
        out_shape=jax.ShapeDtypeStruct((depth, b, n6), F32),
        compiler_params=_cparams(("parallel", "parallel")),
        name="adaln_mod",
    )(c, w_ada, b_ada.reshape(depth, 1, n6))


def _bias_tile_kernel(rb_ref, o_ref, *, thr, t):
    h = pl.program_id(0)
    delta = pl.program_id(1)
    row = lax.broadcasted_iota(jnp.int32, (t, t), 0)
    col = lax.broadcasted_iota(jnp.int32, (t, t), 1)
    n = delta * t + row - col
    val = jnp.full((t, t), rb_ref[0, h], F32)
    for k in range(1, REL_BUCKETS):
        val = jnp.where(n >= thr[k], rb_ref[k, h], val)
    o_ref[0, 0] = val


def _bias_tiles(rel_bias, t):
    thr, bucket = _bucket_thresholds()
    far_bucket = int(bucket[t + 1])
    assert np.all(bucket[t + 1:] == far_bucket)
    tiles = pl.pallas_call(
        functools.partial(_bias_tile_kernel, thr=thr, t=t),
        grid=(DIFF_HEADS, 2),
        in_specs=[pl.BlockSpec(memory_space=pltpu.SMEM)],
        out_specs=pl.BlockSpec((1, 1, t, t), lambda h, d: (h, d, 0, 0)),
        out_shape=jax.ShapeDtypeStruct((DIFF_HEADS, 2, t, t), F32),
        compiler_params=_cparams(("parallel", "parallel")),
        name="rel_bias_tiles",
    )(rel_bias)
    return tiles, rel_bias[far_bucket]


def _rms(x, g):
    return x * lax.rsqrt(jnp.mean(x * x, axis=-1, keepdims=True) + NORM_EPS) * g


def _inproj_kernel(x_ref, mod_ref, g_ref, win_ref, bm_ref, qg_ref, kvg_ref, wq2_ref, wkv2_ref, tab_ref,
                   od_ref, of_ref, om_ref, fo_ref, carry_ref, *, d_model, ts):
    s_idx = pl.program_id(1)
    x = x_ref[0]
    sh = mod_ref[0, :, 0:d_model]
    sc = mod_ref[0, :, d_model:2 * d_model]
    h = _rms(x, g_ref[...]) * (1.0 + sc) + sh
    hb = h.astype(BF16)

    def proj(lo, hi):
        return jnp.dot(hb, win_ref[:, lo:hi], preferred_element_type=F32)

    pd = proj(OFF_D, OFF_F)
    lane_d = lax.broadcasted_iota(jnp.int32, (1, 3 * DIFF_WIDTH), 1)
    pd = pd * jnp.where(lane_d < DIFF_WIDTH, DIFF_QK_DIM ** -0.5, 1.0)
    od_ref[0] = pd.astype(BF16)

    pf = proj(OFF_F, OFF_CQ)
    lane_f = lax.broadcasted_iota(jnp.int32, (1, 3 * FOX_WIDTH), 1)
    pf = pf * jnp.where(lane_f < FOX_WIDTH, FOX_HEAD_DIM ** -0.5, 1.0)
    of_ref[0] = pf.astype(BF16)

    cq = _rms(proj(OFF_CQ, OFF_CKV), qg_ref[...]).astype(BF16)
    qa = jnp.dot(cq, wq2_ref[...], preferred_element_type=F32)
    ckv = _rms(proj(OFF_CKV, OFF_KR), kvg_ref[...]).astype(BF16)
    kva = jnp.dot(ckv, wkv2_ref[...], preferred_element_type=F32)
    kr = proj(OFF_KR, OFF_MISC)
    cq_t = tab_ref[:, 0:LANES]
    sq_t = tab_ref[:, LANES:2 * LANES]
    ck_t = tab_ref[:, 2 * LANES:3 * LANES]
    sk_t = tab_ref[:, 3 * LANES:4 * LANES]
    krope = kr[:, 0:LANES] * ck_t + kr[:, LANES:2 * LANES] * sk_t
    for hh in range(MLA_HEADS):
        lo, hi = hh * LANES, (hh + 1) * LANES
        qh = qa[:, lo:hi] * cq_t + qa[:, MLA_QK_PAD + lo:MLA_QK_PAD + hi] * sq_t
        om_ref[0, :, lo:hi] = qh.astype(BF16)
        om_ref[0, :, MLA_QK_PAD + lo:MLA_QK_PAD + hi] = (kva[:, lo:hi] + krope).astype(BF16)
    om_ref[0, :, 2 * MLA_QK_PAD:2 * MLA_QK_PAD + MLA_WIDTH] = kva[:, MLA_QK_PAD:].astype(BF16)

    @pl.when(s_idx == 0)
    def _():
        carry_ref[...] = jnp.zeros_like(carry_ref)

    z = proj(OFF_MISC, IN_PAD) + bm_ref[...]
    logf = jnp.minimum(z, 0.0) - jnp.log(1.0 + jnp.exp(-jnp.abs(z)))
    row = lax.broadcasted_iota(jnp.int32, (ts, ts), 0)
    col = lax.broadcasted_iota(jnp.int32, (ts, ts), 1)
    tri = jnp.where(row >= col, 1.0, 0.0).astype(F32)
    cum = jnp.dot(tri, logf, preferred_element_type=F32, precision=lax.Precision.HIGHEST) + carry_ref[0:1, :]
    fo_ref[0] = cum
    carry_ref[...] = jnp.broadcast_to(cum[ts - 1:ts, :], carry_ref.shape)


def _inproj(x, mod_l, g, win_p, b_misc, qg, kvg, wq2, wkv2, tab, ts):
    b, s, d = x.shape
    n6 = mod_l.shape[-1]
    wd_, wf_, wm_ = 3 * DIFF_WIDTH, 3 * FOX_WIDTH, 2 * MLA_QK_PAD + MLA_WIDTH
    const2 = lambda bi, si: (0, 0)
    return pl.pallas_call(
        functools.partial(_inproj_kernel, d_model=d, ts=ts),
        grid=(b, s // ts),
        in_specs=[
            pl.BlockSpec((1, ts, d), lambda bi, si: (bi, si, 0)),
            pl.BlockSpec((1, 1, n6), lambda bi, si: (bi, 0, 0)),
            pl.BlockSpec((1, d), const2),
            pl.BlockSpec((d, IN_PAD), const2),
            pl.BlockSpec((1, LANES), const2),
            pl.BlockSpec((1, MLA_Q_RANK), const2),
            pl.BlockSpec((1, MLA_KV_RANK), const2),
            pl.BlockSpec((MLA_Q_RANK, 2 * MLA_QK_PAD), const2),
            pl.BlockSpec((MLA_KV_RANK, MLA_QK_PAD + MLA_WIDTH), const2),
            pl.BlockSpec((ts, 4 * LANES), lambda bi, si: (si, 0)),
        ],
        out_specs=[
            pl.BlockSpec((1, ts, wd_), lambda bi, si: (bi, si, 0)),
            pl.BlockSpec((1, ts, wf_), lambda bi, si: (bi, si, 0)),
            pl.BlockSpec((1, ts, wm_), lambda bi, si: (bi, si, 0)),
            pl.BlockSpec((1, ts, LANES), lambda bi, si: (bi, si, 0)),
        ],
        out_shape=[
            jax.ShapeDtypeStruct((b, s, wd_), BF16),
            jax.ShapeDtypeStruct((b, s, wf_), BF16),
            jax.ShapeDtypeStruct((b, s, wm_), BF16),
            jax.ShapeDtypeStruct((b, s, LANES), F32),
        ],
        scratch_shapes=[pltpu.VMEM((8, LANES), F32)],
        compiler_params=_cparams(("parallel", "arbitrary")),
        name="norm_inproj",
    )(x, mod_l.reshape(b, 1, n6), g.reshape(1, d), win_p, b_misc, qg.reshape(1, -1), kvg.reshape(1, -1),
      wq2, wkv2, tab)


def _init_state(m_ref, l_ref, acc_ref):
    m_ref[...] = jnp.full(m_ref.shape, -jnp.inf, F32)
    l_ref[...] = jnp.zeros(l_ref.shape, F32)
    acc_ref[...] = jnp.zeros(acc_ref.shape, F32)


def _chunk(q, load_k, load_v, state, j, add, masked, t):
    m_ref, l_ref, acc_ref = state
    start = pl.multiple_of(j * t, t)
    k = load_k(start)
    s = lax.dot_general(q, k, (((1,), (1,)), ((), ())), preferred_element_type=F32)
    s = add(s, start)
    if masked:
        row = lax.broadcasted_iota(jnp.int32, (t, t), 0)
        col = lax.broadcasted_iota(jnp.int32, (t, t), 1)
        s = jnp.where(row >= col, s, -jnp.inf)
    m_prev = m_ref[...]
    m_new = jnp.maximum(m_prev, jnp.max(s, axis=1, keepdims=True))
    alpha = jnp.exp(m_prev - m_new)
    p = jnp.exp(s - m_new)
    l_ref[...] = alpha * l_ref[...] + jnp.sum(p, axis=1, keepdims=True)
    acc_ref[...] = alpha * acc_ref[...] + jnp.dot(p.astype(BF16), load_v(start), preferred_element_type=F32)
    m_ref[...] = m_new


def _plain_map(q, load_k, load_v, state, qi, add, t):
    _init_state(*state)

    def body(j, c):
        _chunk(q, load_k, load_v, state, j, add, False, t)
        return c

    lax.fori_loop(0, qi, body, 0)
    _chunk(q, load_k, load_v, state, qi, add, True, t)
    return state[2][...] / state[1][...]


def _diff_attn_kernel(far_ref, q_ref, k_ref, v_ref, bias_ref, lam_ref, g_ref, o_ref, m_ref, l_ref, acc_ref,
                      *, t, lam_init):
    p = pl.program_id(1)
    qi = pl.program_id(2)
    state = (m_ref, l_ref, acc_ref)
    q = q_ref[0]
    lane = lax.broadcasted_iota(jnp.int32, (1, LANES), 1)
    lp = lam_ref[...]
    lam = (jnp.exp(jnp.sum(lp[0:1] * lp[1:2], axis=1, keepdims=True))
           - jnp.exp(jnp.sum(lp[2:3] * lp[3:4], axis=1, keepdims=True)) + lam_init)
    load_k = lambda start: k_ref[0, pl.ds(start, t), :]
    load_v = lambda start: v_ref[0, pl.ds(start, t), :]

    def one_map(i, mp):
        lo = (2 * i + mp) * DIFF_QK_DIM
        qm = jnp.where((lane >= lo) & (lane < lo + DIFF_QK_DIM), q, jnp.zeros_like(q))
        far = far_ref[2 * p + i]
        _init_state(*state)

        def body(j, c):
            _chunk(qm, load_k, load_v, state, j, lambda s, st: s + far, False, t)
            return c

        lax.fori_loop(0, jnp.maximum(qi - 1, 0), body, 0)

        @pl.when(qi >= 1)
        def _():
            _chunk(qm, load_k, load_v, state, qi - 1, lambda s, st: s + bias_ref[i, 1], False, t)

        _chunk(qm, load_k, load_v, state, qi, lambda s, st: s + bias_ref[i, 0], True, t)
        return acc_ref[...] / l_ref[...]

    heads = []
    for i in range(2):
        o0 = one_map(i, 0)
        o1 = one_map(i, 1)
        heads.append(o0 - lam * o1)
    first = lane < DIFF_V_DIM
    o = jnp.where(first, heads[0], heads[1])
    sq = o * o
    ss0 = jnp.sum(jnp.where(first, sq, 0.0), axis=1, keepdims=True)
    ss1 = jnp.sum(jnp.where(first, 0.0, sq), axis=1, keepdims=True)
    ms = jnp.where(first, ss0, ss1) * (1.0 / DIFF_V_DIM)
    out = o * lax.rsqrt(ms + NORM_EPS) * g_ref[...] * (1.0 - lam_init)
    o_ref[0] = out.astype(o_ref.dtype)


def _diff_attention(pd, bias_tiles, far, diff_lambda, subln_g, lam_init, t):
    b, s, _ = pd.shape
    npair = DIFF_HEADS // 2
    g2 = jnp.concatenate([subln_g, subln_g]).reshape(1, LANES)
    return pl.pallas_call(
        functools.partial(_diff_attn_kernel, t=t, lam_init=lam_init),
        grid=(b, npair, s // t),
        in_specs=[
            pl.BlockSpec(memory_space=pltpu.SMEM),
            pl.BlockSpec((1, t, LANES), lambda bi, p, qi: (bi, qi, p)),
            pl.BlockSpec((1, s, LANES), lambda bi, p, qi: (bi, 0, npair + p)),
            pl.BlockSpec((1, s, LANES), lambda bi, p, qi: (bi, 0, 2 * npair + p)),
            pl.BlockSpec((2, 2, t, t), lambda bi, p, qi: (p, 0, 0, 0)),
            pl.BlockSpec((4, DIFF_QK_DIM), lambda bi, p, qi: (0, 0)),
            pl.BlockSpec((1, LANES), lambda bi, p, qi: (0, 0)),
        ],
        out_specs=pl.BlockSpec((1, t, LANES), lambda bi, p, qi: (bi, qi, p)),
        out_shape=jax.ShapeDtypeStruct((b, s, DIFF_WIDTH), BF16),
        scratch_shapes=[pltpu.VMEM((t, 1), F32), pltpu.VMEM((t, 1), F32), pltpu.VMEM((t, LANES), F32)],
        compiler_params=_cparams(("parallel", "parallel", "arbitrary")),
        name="diff_attention",
    )(far, pd, pd, pd, bias_tiles, diff_lambda, g2)


def _fox_attn_kernel(q_ref, k_ref, v_ref, fcol_ref, frow_ref, o_ref, m_ref, l_ref, acc_ref, *, t):
    p = pl.program_id(1)
    qi = pl.program_id(2)
    state = (m_ref, l_ref, acc_ref)
    q = q_ref[0]
    lane = lax.broadcasted_iota(jnp.int32, (1, LANES), 1)
    fcol_all = fcol_ref[0]
    load_k = lambda start: k_ref[0, pl.ds(start, t), :]
    load_v = lambda start: v_ref[0, pl.ds(start, t), :]
    outs = []
    for i in range(2):
        lo = i * FOX_HEAD_DIM
        qm = jnp.where((lane >= lo) & (lane < lo + FOX_HEAD_DIM), q, jnp.zeros_like(q))
        fc = jnp.sum(jnp.where(lane == 2 * p + i, fcol_all, 0.0), axis=1, keepdims=True)

        def add(s, start, i=i, fc=fc):
            return s + (fc - frow_ref[0, i, :, pl.ds(start, t)])

        outs.append(_plain_map(qm, load_k, load_v, state, qi, add, t))
    o_ref[0] = jnp.where(lane < FOX_HEAD_DIM, outs[0], outs[1]).astype(o_ref.dtype)


def _fox_attention(pf, fcol, frow, t):
    b, s, _ = pf.shape
    npair = FOX_HEADS // 2
    return pl.pallas_call(
        functools.partial(_fox_attn_kernel, t=t),
        grid=(b, npair, s // t),
        in_specs=[
            pl.BlockSpec((1, t, LANES), lambda bi, p, qi: (bi, qi, p)),
            pl.BlockSpec((1, s, LANES), lambda bi, p, qi: (bi, 0, npair + p)),
            pl.BlockSpec((1, s, LANES), lambda bi, p, qi: (bi, 0, 2 * npair + p)),
            pl.BlockSpec((1, t, LANES), lambda bi, p, qi: (bi, qi, 0)),
            pl.BlockSpec((1, 2, 1, s), lambda bi, p, qi: (bi, p, 0, 0)),
        ],
        out_specs=pl.BlockSpec((1, t, LANES), lambda bi, p, qi: (bi, qi, p)),
        out_shape=jax.ShapeDtypeStruct((b, s, FOX_WIDTH), BF16),
        scratch_shapes=[pltpu.VMEM((t, 1), F32), pltpu.VMEM((t, 1), F32), pltpu.VMEM((t, LANES), F32)],
        compiler_params=_cparams(("parallel", "parallel", "arbitrary")),
        name="fox_attention",
    )(pf, pf, pf, fcol, frow)


def _mla_attn_kernel(q_ref, k_ref, v_ref, o_ref, m_ref, l_ref, acc_ref, *, t):
    qi = pl.program_id(2)
    state = (m_ref, l_ref, acc_ref)
    lane = lax.broadcasted_iota(jnp.int32, (1, LANES), 1)
    load_v = lambda start: v_ref[0, pl.ds(start, t), :]
    outs = []
    for i in range(2):
        q = q_ref[0, :, i * LANES:(i + 1) * LANES]
        load_k = lambda start, i=i: k_ref[0, pl.ds(start, t), i * LANES:(i + 1) * LANES]
        outs.append(_plain_map(q, load_k, load_v, state, qi, lambda s, st: s, t))
    o_ref[0] = jnp.where(lane < MLA_V_DIM, outs[0], outs[1]).astype(o_ref.dtype)


def _mla_attention(pm, t):
    b, s, _ = pm.shape
    npair = MLA_HEADS // 2
    kblk = MLA_QK_PAD // (2 * LANES)
    vblk = 2 * MLA_QK_PAD // LANES
    return pl.pallas_call(
        functools.partial(_mla_attn_kernel, t=t),
        grid=(b, npair, s // t),
        in_specs=[
            pl.BlockSpec((1, t, 2 * LANES), lambda bi, p, qi: (bi, qi, p)),
            pl.BlockSpec((1, s, 2 * LANES), lambda bi, p, qi: (bi, 0, kblk + p)),
            pl.BlockSpec((1, s, LANES), lambda bi, p, qi: (bi, 0, vblk + p)),
        ],
        out_specs=pl.BlockSpec((1, t, LANES), lambda bi, p, qi: (bi, qi, p)),
        out_shape=jax.ShapeDtypeStruct((b, s, MLA_WIDTH), BF16),
        scratch_shapes=[pltpu.VMEM((t, 1), F32), pltpu.VMEM((t, 1), F32), pltpu.VMEM((t, LANES), F32)],
        compiler_params=_cparams(("parallel", "parallel", "arbitrary")),
        name="mla_attention",
    )(pm, pm, pm)


def _outproj_kernel(*refs, d_model, with_router):
    if with_router:
        od_ref, of_ref, om_ref, w_ref, x_ref, mod_ref, g_ref, rw_ref, xo_ref, h_ref, cmb_ref = refs
    else:
        od_ref, of_ref, om_ref, w_ref, x_ref, mod_ref, g_ref, xo_ref, h_ref = refs
    d = d_model
    mix = jnp.dot(od_ref[0], w_ref[0:DIFF_WIDTH, :], preferred_element_type=F32)
    mix += jnp.dot(of_ref[0], w_ref[DIFF_WIDTH:DIFF_WIDTH + FOX_WIDTH, :], preferred_element_type=F32)
    mix += jnp.dot(om_ref[0], w_ref[DIFF_WIDTH + FOX_WIDTH:, :], preferred_element_type=F32)
    g_a = mod_ref[0, :, 2 * d:3 * d]
    sh_f = mod_ref[0, :, 3 * d:4 * d]
    sc_f = mod_ref[0, :, 4 * d:5 * d]
    xn = x_ref[0] + g_a * mix
    xo_ref[0] = xn
    h = _rms(xn, g_ref[...]) * (1.0 + sc_f) + sh_f
    h_ref[0] = h.astype(BF16)
    if with_router:
        logits = jnp.dot(h, rw_ref[...], preferred_element_type=F32, precision=lax.Precision.HIGHEST)
        lane = lax.broadcasted_iota(jnp.int32, logits.shape, 1).astype(F32)
        neg = -jnp.inf
        lg = jnp.where(lane < N_EXPERTS, logits, neg)
        m1 = jnp.max(lg, axis=1, keepdims=True)
        i1 = jnp.min(jnp.where(lg == m1, lane, float(LANES)), axis=1, keepdims=True)
        lg2 = jnp.where(lane == i1, neg, lg)
        m2 = jnp.max(lg2, axis=1, keepdims=True)
        i2 = jnp.min(jnp.where(lg2 == m2, lane, float(LANES)), axis=1, keepdims=True)
        e2 = jnp.exp(m2 - m1)
        w1 = 1.0 / (1.0 + e2)
        w2 = e2 / (1.0 + e2)
        cmb_ref[0] = jnp.where(lane == i1, w1, 0.0) + jnp.where(lane == i2, w2, 0.0)


def _outproj(od, of, om, w_out_b, x, mod_l, g, router_pad, ts):
    b, s, d = x.shape
    n6 = mod_l.shape[-1]
    with_router = router_pad is not None
    tok = lambda bi, si: (bi, si, 0)
    const2 = lambda bi, si: (0, 0)
    in_specs = [
        pl.BlockSpec((1, ts, DIFF_WIDTH), tok),
        pl.BlockSpec((1, ts, FOX_WIDTH), tok),
        pl.BlockSpec((1, ts, MLA_WIDTH), tok),
        pl.BlockSpec((d, d), const2),
        pl.BlockSpec((1, ts, d), tok),
        pl.BlockSpec((1, 1, n6), lambda bi, si: (bi, 0, 0)),
        pl.BlockSpec((1, d), const2),
    ]
    args = [od, of, om, w_out_b, x, mod_l.reshape(b, 1, n6), g.reshape(1, d)]
    out_specs = [pl.BlockSpec((1, ts, d), tok), pl.BlockSpec((1, ts, d), tok)]
    out_shape = [jax.ShapeDtypeStruct((b, s, d), F32), jax.ShapeDtypeStruct((b, s, d), BF16)]
    if with_router:
        in_specs.append(pl.BlockSpec((d, LANES), const2))
        args.append(router_pad)
        out_specs.append(pl.BlockSpec((1, ts, LANES), tok))
        out_shape.append(jax.ShapeDtypeStruct((b, s, LANES), F32))
    return pl.pallas_call(
        functools.partial(_outproj_kernel, d_model=d, with_router=with_router),
        grid=(b, s // ts),
        in_specs=in_specs,
        out_specs=out_specs,
        out_shape=out_shape,
        compiler_params=_cparams(("parallel", "parallel")),
        name="outproj_router" if with_router else "outproj",
    )(*args)


def _finish(x_ref, mod_ref, acc, fg_ref, o_ref, d, final):
    g_f = mod_ref[0, :, 5 * d:6 * d]
    xo = x_ref[0] + g_f * acc
    if final:
        xo = _rms(xo, fg_ref[...])
    o_ref[0] = xo


def _ffn_kernel(h_ref, wg_ref, wu_ref, wd_ref, x_ref, mod_ref, fg_ref, o_ref, acc_ref, *, d_model, final):
    j = pl.program_id(2)

    @pl.when(j == 0)
    def _():
        acc_ref[...] = jnp.zeros_like(acc_ref)

    h = h_ref[0]
    a = jnp.dot(h, wg_ref[...], preferred_element_type=F32)
    u = jnp.dot(h, wu_ref[...], preferred_element_type=F32)
    act = (_silu(a) * u).astype(BF16)
    acc_ref[...] += jnp.dot(act, wd_ref[...], preferred_element_type=F32)

    @pl.when(j == pl.num_programs(2) - 1)
    def _():
        _finish(x_ref, mod_ref, acc_ref[...], fg_ref, o_ref, d_model, final)


def _ffn_dense(h, wg, wu, wd, x, mod_l, final_g, final, tm, tf):
    b, s, d = x.shape
    n6 = mod_l.shape[-1]
    dff = wg.shape[1]
    tok = lambda bi, si, j: (bi, si, 0)
    return pl.pallas_call(
        functools.partial(_ffn_kernel, d_model=d, final=final),
        grid=(b, s // tm, dff // tf),
        in_specs=[
            pl.BlockSpec((1, tm, d), tok),
            pl.BlockSpec((d, tf), lambda bi, si, j: (0, j)),
            pl.BlockSpec((d, tf), lambda bi, si, j: (0, j)),
            pl.BlockSpec((tf, d), lambda bi, si, j: (j, 0)),
            pl.BlockSpec((1, tm, d), tok),
            pl.BlockSpec((1, 1, n6), lambda bi, si, j: (bi, 0, 0)),
            pl.BlockSpec((1, d), lambda bi, si, j: (0, 0)),
        ],
        out_specs=pl.BlockSpec((1, tm, d), tok),
        out_shape=jax.ShapeDtypeStruct((b, s, d), F32),
        scratch_shapes=[pltpu.VMEM((tm, d), F32)],
        compiler_params=_cparams(("parallel", "parallel", "arbitrary")),
        name="ffn_dense",
    )(h, wg, wu, wd, x, mod_l.reshape(b, 1, n6), final_g.reshape(1, d))


def _moe_kernel(h_ref, cmb_ref, wg_ref, wu_ref, wd_ref, x_ref, mod_ref, fg_ref, o_ref, acc_ref, *, d_model, final):
    e = pl.program_id(2)
    j = pl.program_id(3)

    @pl.when((e == 0) & (j == 0))
    def _():
        acc_ref[...] = jnp.zeros_like(acc_ref)

    h = h_ref[0]
    a = jnp.dot(h, wg_ref[0], preferred_element_type=F32)
    u = jnp.dot(h, wu_ref[0], preferred_element_type=F32)
    act = (_silu(a) * u).astype(BF16)
    lane = lax.broadcasted_iota(jnp.int32, (1, LANES), 1)
    ce = jnp.sum(jnp.where(lane == e, cmb_ref[0], 0.0), axis=1, keepdims=True)
    acc_ref[...] += ce * jnp.dot(act, wd_ref[0], preferred_element_type=F32)

    @pl.when((e == pl.num_programs(2) - 1) & (j == pl.num_programs(3) - 1))
    def _():
        _finish(x_ref, mod_ref, acc_ref[...], fg_ref, o_ref, d_model, final)


def _moe_dense(h, cmb, wg, wu, wd, x, mod_l, final_g, final, tm, tf):
    b, s, d = x.shape
    n6 = mod_l.shape[-1]
    ne, _, dff = wg.shape
    tok = lambda bi, si, e, j: (bi, si, 0)
    return pl.pallas_call(
        functools.partial(_moe_kernel, d_model=d, final=final),
        grid=(b, s // tm, ne, dff // tf),
        in_specs=[
            pl.BlockSpec((1, tm, d), tok),
            pl.BlockSpec((1, tm, LANES), tok),
            pl.BlockSpec((1, d, tf), lambda bi, si, e, j: (e, 0, j)),
            pl.BlockSpec((1, d, tf), lambda bi, si, e, j: (e, 0, j)),
            pl.BlockSpec((1, tf, d), lambda bi, si, e, j: (e, j, 0)),
            pl.BlockSpec((1, tm, d), tok),
            pl.BlockSpec((1, 1, n6), lambda bi, si, e, j: (bi, 0, 0)),
            pl.BlockSpec((1, d), lambda bi, si, e, j: (0, 0)),
        ],
        out_specs=pl.BlockSpec((1, tm, d), tok),
        out_shape=jax.ShapeDtypeStruct((b, s, d), F32),
        scratch_shapes=[pltpu.VMEM((tm, d), F32)],
        compiler_params=_cparams(("parallel", "parallel", "arbitrary", "arbitrary")),
        name="moe_dense",
    )(h, cmb, wg, wu, wd, x, mod_l.reshape(b, 1, n6), final_g.reshape(1, d))


def _prep_w_in(w_in_l):
    d = w_in_l.shape[0]
    sizes = [DIFF_WIDTH, DIFF_WIDTH, DIFF_WIDTH, FOX_WIDTH, FOX_WIDTH, FOX_WIDTH, FOX_HEADS,
             MLA_Q_RANK, MLA_KV_RANK, MLA_ROPE_DIM]
    offs = np.cumsum([0] + sizes)
    sec = [w_in_l[:, offs[i]:offs[i + 1]] for i in range(len(sizes))]
    dq, dk, dv, fq, fk, fv, ff, cq, ckv, kr = sec
    half = MLA_ROPE_DIM // 2
    z = lambda n: jnp.zeros((d, n), w_in_l.dtype)
    kr_sw = jnp.concatenate([kr[:, half:], kr[:, :half]], axis=1)
    pad_r = LANES - MLA_NOPE_DIM - MLA_ROPE_DIM
    cols = [dq, dk, dv, fq, fk, fv, cq, ckv,
            z(MLA_NOPE_DIM), kr, z(pad_r), z(MLA_NOPE_DIM), kr_sw, z(pad_r),
            ff, z(LANES - FOX_HEADS)]
    return jnp.concatenate(cols, axis=1).astype(BF16)


def _prep_w_uq(w_uq_l):
    r = w_uq_l.shape[0]
    w = w_uq_l.reshape(r, MLA_HEADS, MLA_NOPE_DIM + MLA_ROPE_DIM)
    nope, rope = w[..., :MLA_NOPE_DIM], w[..., MLA_NOPE_DIM:]
    half = MLA_ROPE_DIM // 2
    rope_sw = jnp.concatenate([rope[..., half:], rope[..., :half]], axis=-1)
    pad_r = LANES - MLA_NOPE_DIM - MLA_ROPE_DIM
    zr = jnp.zeros((r, MLA_HEADS, pad_r), w.dtype)
    zn = jnp.zeros((r, MLA_HEADS, MLA_NOPE_DIM), w.dtype)
    plain = jnp.concatenate([nope, rope, zr], axis=-1).reshape(r, MLA_QK_PAD)
    swapped = jnp.concatenate([zn, rope_sw, zr], axis=-1).reshape(r, MLA_QK_PAD)
    return jnp.concatenate([plain, swapped], axis=1).astype(BF16)


def _prep_w_ukv(w_ukv_l):
    r = w_ukv_l.shape[0]
    w = w_ukv_l.reshape(r, MLA_HEADS, MLA_NOPE_DIM + MLA_V_DIM)
    k_nope, v = w[..., :MLA_NOPE_DIM], w[..., MLA_NOPE_DIM:]
    zk = jnp.zeros((r, MLA_HEADS, LANES - MLA_NOPE_DIM), w.dtype)
    k_pad = jnp.concatenate([k_nope, zk], axis=-1).reshape(r, MLA_QK_PAD)
    return jnp.concatenate([k_pad, v.reshape(r, MLA_WIDTH)], axis=1).astype(BF16)


def _rope_tables(seq):
    half = MLA_ROPE_DIM // 2
    freqs = ROPE_THETA ** (-jnp.arange(half, dtype=F32) / half)
    ang = jnp.arange(seq, dtype=F32)[:, None] * freqs[None, :]
    cos, sin = jnp.cos(ang), jnp.sin(ang)
    scale = (MLA_NOPE_DIM + MLA_ROPE_DIM) ** -0.5
    pad_r = LANES - MLA_NOPE_DIM - MLA_ROPE_DIM
    ones = jnp.ones((seq, MLA_NOPE_DIM), F32)
    zn = jnp.zeros((seq, MLA_NOPE_DIM), F32)
    zr = jnp.zeros((seq, pad_r), F32)
    c_q = jnp.concatenate([ones, cos, cos, zr], axis=1) * scale
    s_q = jnp.concatenate([zn, -sin, sin, zr], axis=1) * scale
    c_k = jnp.concatenate([zn, cos, cos, zr], axis=1)
    s_k = jnp.concatenate([zn, -sin, sin, zr], axis=1)
    return jnp.concatenate([c_q, s_q, c_k, s_k], axis=1)


def _lambda_init(layer_idx):
    return 0.8 - 0.6 * math.exp(-0.3 * layer_idx)


def _pick(n, cands):
    for c in cands:
        if n % c == 0:
            return c
    return n


def kernel(x, c, w_ada, b_ada, attn_norm, ffn_norm, w_in, b_forget, diff_lambda, diff_subln, rel_bias, mla_q_norm,
           mla_kv_norm, w_uq, w_ukv, w_out, ffn_w_gate, ffn_w_up, ffn_w_down, router_w, moe_w_gate, moe_w_up,
           moe_w_down, final_norm):
    b, s, d = x.shape
    depth = w_ada.shape[0]
    t = ATTN_TQ
    ts_a = _pick(s, (512, 256))
    ts_w = _pick(s, (512, 256))
    tm = _pick(s, (512, 256))

    mod = _modulation(c, w_ada, b_ada)
    bias_tiles, far = _bias_tiles(rel_bias, t)
    tab = _rope_tables(s)

    for l in range(depth):
        mod_l = mod[l]
        win_p = _prep_w_in(w_in[l])
        b_misc = jnp.concatenate([b_forget[l], jnp.zeros((LANES - FOX_HEADS,), F32)]).reshape(1, LANES)
        pd, pf, pm, fcum = _inproj(x, mod_l, attn_norm[l], win_p, b_misc, mla_q_norm[l], mla_kv_norm[l],
                                   _prep_w_uq(w_uq[l]), _prep_w_ukv(w_ukv[l]), tab, ts_a)
        frow = jnp.transpose(fcum[:, :, :FOX_HEADS], (0, 2, 1)).reshape(b, FOX_HEADS, 1, s)
        od = _diff_attention(pd, bias_tiles, far, diff_lambda[l], diff_subln[l], _lambda_init(l), t)
        of = _fox_attention(pf, fcum, frow, t)
        om = _mla_attention(pm, t)

        is_moe = l % 2 == 1
        i = l // 2
        final = l == depth - 1
        if is_moe:
            rpad = jnp.concatenate([router_w[i], jnp.zeros((d, LANES - N_EXPERTS), F32)], axis=1)
            x, h2, cmb = _outproj(od, of, om, w_out[l].astype(BF16), x, mod_l, ffn_norm[l], rpad, ts_w)
            dff = moe_w_gate.shape[-1]
            x = _moe_dense(h2, cmb, moe_w_gate[i].astype(BF16), moe_w_up[i].astype(BF16),
                           moe_w_down[i].astype(BF16), x, mod_l, final_norm, final, tm, _pick(dff, (896, 512, 256)))
        else:
            x, h2 = _outproj(od, of, om, w_out[l].astype(BF16), x, mod_l, ffn_norm[l], None, ts_w)
            dff = ffn_w_gate.shape[-1]
            x = _ffn_dense(h2, ffn_w_gate[i].astype(BF16), ffn_w_up[i].astype(BF16), ffn_w_down[i].astype(BF16),
                           x, mod_l, final_norm, final, tm, _pick(dff, (1408, 512, 256)))
    return x
```

```python
import functools
import math

import numpy as np
import jax
import jax.numpy as jnp
from jax import lax
from jax.experimental import pallas as pl
from jax.experimental.pallas import tpu as pltpu

F32 = jnp.float32
BF16 = jnp.bfloat16

DIFF_HEADS = 4
DIFF_QK_DIM = 32
DIFF_V_DIM = 64
FOX_HEADS = 6
FOX_HEAD_DIM = 64
MLA_HEADS = 6
MLA_Q_RANK = 256
MLA_KV_RANK = 128
MLA_NOPE_DIM = 64
MLA_ROPE_DIM = 32
MLA_V_DIM = 64
ROPE_THETA = 10000.0
REL_BUCKETS = 32
REL_MAX_DIST = 128
DIFF_WIDTH = DIFF_HEADS * DIFF_V_DIM
FOX_WIDTH = FOX_HEADS * FOX_HEAD_DIM
MLA_WIDTH = MLA_HEADS * MLA_V_DIM
N_EXPERTS = 8
NORM_EPS = 1e-6
LOG2E = math.log2(math.e)

LANES = 128
VMEM_LIMIT_BYTES = 56 * 1024 * 1024

OFF_D = 0
OFF_F = OFF_D + 3 * DIFF_WIDTH
OFF_CQ = OFF_F + 3 * FOX_WIDTH
OFF_CKV = OFF_CQ + MLA_Q_RANK
OFF_KR = OFF_CKV + MLA_KV_RANK
OFF_KRS = OFF_KR + LANES
OFF_MISC = OFF_KRS + LANES
IN_PAD = OFF_MISC + LANES
MLA_QK_PAD = MLA_HEADS * LANES
FOX_PAIRS = FOX_HEADS // 2
FOX_QK_PAD = FOX_PAIRS * 2 * LANES
GATE_PIECES = 3

ATTN_T = 256


def _cparams(sem):
    return pltpu.CompilerParams(dimension_semantics=sem, vmem_limit_bytes=VMEM_LIMIT_BYTES)


def _sigmoid(x):
    return 1.0 / (1.0 + jnp.exp(-x))


def _silu(x):
    return x * _sigmoid(x)


def _bucket_thresholds():
    max_exact = REL_BUCKETS // 2
    n = np.arange(0, 4096)
    nf = np.maximum(n, 1).astype(np.float32)
    large = max_exact + (np.log(nf / np.float32(max_exact)) / np.float32(math.log(REL_MAX_DIST / max_exact))
                         * np.float32(REL_BUCKETS - max_exact)).astype(np.int32)
    large = np.minimum(large, REL_BUCKETS - 1)
    bucket = np.where(n < max_exact, n, large)
    assert np.all(np.diff(bucket) >= 0)
    thr = [int(np.argmax(bucket >= k)) for k in range(REL_BUCKETS)]
    assert all(bucket[thr[k]] >= k for k in range(REL_BUCKETS))
    return thr, bucket


def _mod_kernel(c_ref, w_ref, b_ref, o_ref):
    ca = _silu(c_ref[...]).astype(BF16)
    w = w_ref[0].astype(BF16)
    o_ref[0] = jnp.dot(ca, w, preferred_element_type=F32) + b_ref[0]


def _modulation(c, w_ada, b_ada):
    depth, d, n6 = w_ada.shape
    b = c.shape[0]
    tn = n6 // 4
    return pl.pallas_call(
        _mod_kernel,
        grid=(depth, n6 // tn),
        in_specs=[
            pl.BlockSpec((b, d), lambda l, j: (0, 0)),
            pl.BlockSpec((1, d, tn), lambda l, j: (l, 0, j)),
            pl.BlockSpec((1, 1, tn), lambda l, j: (l, 0, j)),
        ],
        out_specs=pl.BlockSpec((1, b, tn), lambda l, j: (l, 0, j)),
        out_shape=jax.ShapeDtypeStruct((depth, b, n6), F32),
        compiler_params=_cparams(("parallel", "parallel")),
        name="adaln_mod",
    )(c, w_ada, b_ada.reshape(depth, 1, n6))


def _bias_tile_kernel(rb_ref, o_ref, *, thr, t, far_bucket):
    h = pl.program_id(0)
    delta = pl.program_id(1)
    key = lax.broadcasted_iota(jnp.int32, (t, t), 0)
    qry = lax.broadcasted_iota(jnp.int32, (t, t), 1)
    n = delta * t + qry - key
    val = jnp.full((t, t), rb_ref[0, h], F32)
    for k in range(1, REL_BUCKETS):
        val = jnp.where(n >= thr[k], rb_ref[k, h], val)
    val = (val - rb_ref[far_bucket, h]) * LOG2E
    o_ref[0, 0] = jnp.where(n >= 0, val, -jnp.inf)


def _bias_tiles(rel_bias, t):
    thr, bucket = _bucket_thresholds()
    far_bucket = int(bucket[t + 1])
    assert np.all(bucket[t + 1:] == far_bucket)
    return pl.pallas_call(
        functools.partial(_bias_tile_kernel, thr=thr, t=t, far_bucket=far_bucket),
        grid=(DIFF_HEADS, 2),
        in_specs=[pl.BlockSpec(memory_space=pltpu.SMEM)],
        out_specs=pl.BlockSpec((1, 1, t, t), lambda h, d: (h, d, 0, 0)),
        out_shape=jax.ShapeDtypeStruct((DIFF_HEADS, 2, t, t), F32),
        compiler_params=_cparams(("parallel", "parallel")),
        name="rel_bias_tiles",
    )(rel_bias)


def _rms(x, g):
    return x * lax.rsqrt(jnp.mean(x * x, axis=-1, keepdims=True) + NORM_EPS) * g


def _gate_select_constants():
    n_in = GATE_PIECES * LANES
    n_out = FOX_PAIRS * LANES
    sel_k = np.zeros((n_in, n_out), np.float32)
    sel_q = np.zeros((n_in, n_out), np.float32)
    one_k = np.zeros((1, n_out), np.float32)
    one_q = np.zeros((1, n_out), np.float32)
    for h in range(FOX_HEADS):
        p, i = divmod(h, 2)
        base = p * LANES + 2 * GATE_PIECES * i
        for c in range(GATE_PIECES):
            sel_k[c * LANES + h, base + c] = -1.0
            one_k[0, base + GATE_PIECES + c] = 1.0
            one_q[0, base + c] = 1.0
            sel_q[c * LANES + h, base + GATE_PIECES + c] = 1.0
    return sel_k, sel_q, one_k, one_q


def _inproj_kernel(x_ref, mod_ref, g_ref, win_ref, bm_ref, qg_ref, kvg_ref, wq2_ref, wkv2_ref, tab_ref,
                   selk_ref, selq_ref, onek_ref, oneq_ref,
                   qtd_ref, kd_ref, vtd_ref, qtf_ref, kf_ref, vtf_ref, qtm_ref, km_ref, vtm_ref, carry_ref,
                   *, d_model, ts):
    s_idx = pl.program_id(1)
    x = x_ref[0]
    sh = mod_ref[0, :, 0:d_model]
    sc = mod_ref[0, :, d_model:2 * d_model]
    h = _rms(x, g_ref[...]) * (1.0 + sc) + sh
    hb = h.astype(BF16)

    def proj(lo, hi):
        return jnp.dot(hb, win_ref[:, lo:hi], preferred_element_type=F32)

    pd = proj(OFF_D, OFF_F)
    qtd_ref[0] = (pd[:, 0:DIFF_WIDTH] * (DIFF_QK_DIM ** -0.5 * LOG2E)).T.astype(BF16)
    kd_ref[0] = pd[:, DIFF_WIDTH:2 * DIFF_WIDTH].astype(BF16)
    vtd_ref[0] = pd[:, 2 * DIFF_WIDTH:].T.astype(BF16)

    @pl.when(s_idx == 0)
    def _():
        carry_ref[...] = jnp.zeros_like(carry_ref)

    z = proj(OFF_MISC, IN_PAD) + bm_ref[...]
    logf = jnp.minimum(z, 0.0) - jnp.log(1.0 + jnp.exp(-jnp.abs(z)))
    row = lax.broadcasted_iota(jnp.int32, (ts, ts), 0)
    col = lax.broadcasted_iota(jnp.int32, (ts, ts), 1)
    tri = jnp.where(row >= col, 1.0, 0.0).astype(F32)
    cum = jnp.dot(tri, logf, preferred_element_type=F32, precision=lax.Precision.HIGHEST) + carry_ref[0:1, :]
    carry_ref[...] = jnp.broadcast_to(cum[ts - 1:ts, :], carry_ref.shape)
    rem = cum * LOG2E
    pieces = []
    for _ in range(GATE_PIECES):
        piece = rem.astype(BF16)
        pieces.append(piece)
        rem = rem - piece.astype(F32)
    gp = jnp.concatenate(pieces, axis=1)
    gate_k = jnp.dot(gp, selk_ref[...], preferred_element_type=F32) + onek_ref[...]
    gate_q = jnp.dot(gp, selq_ref[...], preferred_element_type=F32) + oneq_ref[...]

    pf = proj(OFF_F, OFF_CQ)
    for p in range(FOX_PAIRS):
        lo, hi = p * LANES, (p + 1) * LANES
        blk = 2 * p * LANES
        qtf_ref[0, blk:blk + LANES, :] = (pf[:, lo:hi] * (FOX_HEAD_DIM ** -0.5 * LOG2E)).T.astype(BF16)
        qtf_ref[0, blk + LANES:blk + 2 * LANES, :] = gate_q[:, lo:hi].T.astype(BF16)
        kf_ref[0, :, blk:blk + LANES] = pf[:, FOX_WIDTH + lo:FOX_WIDTH + hi].astype(BF16)
        kf_ref[0, :, blk + LANES:blk + 2 * LANES] = gate_k[:, lo:hi].astype(BF16)
    vtf_ref[0] = pf[:, 2 * FOX_WIDTH:].T.astype(BF16)

    cq = _rms(proj(OFF_CQ, OFF_CKV), qg_ref[...]).astype(BF16)
    qa = jnp.dot(cq, wq2_ref[...], preferred_element_type=F32)
    ckv = _rms(proj(OFF_CKV, OFF_KR), kvg_ref[...]).astype(BF16)
    kva = jnp.dot(ckv, wkv2_ref[...], preferred_element_type=F32)
    kr = proj(OFF_KR, OFF_MISC)
    cq_t = tab_ref[:, 0:LANES]
    sq_t = tab_ref[:, LANES:2 * LANES]
    ck_t = tab_ref[:, 2 * LANES:3 * LANES]
    sk_t = tab_ref[:, 3 * LANES:4 * LANES]
    krope = kr[:, 0:LANES] * ck_t + kr[:, LANES:2 * LANES] * sk_t
    for hh in range(MLA_HEADS):
        lo, hi = hh * LANES, (hh + 1) * LANES
        qh = qa[:, lo:hi] * cq_t + qa[:, MLA_QK_PAD + lo:MLA_QK_PAD + hi] * sq_t
        qtm_ref[0, lo:hi, :] = qh.T.astype(BF16)
        km_ref[0, :, lo:hi] = (kva[:, lo:hi] + krope).astype(BF16)
    vtm_ref[0] = kva[:, MLA_QK_PAD:].T.astype(BF16)


def _inproj(x, mod_l, g, win_p, b_misc, qg, kvg, wq2, wkv2, tab, ts):
    b, s, d = x.shape
    n6 = mod_l.shape[-1]
    sel_k, sel_q, one_k, one_q = _gate_select_constants()
    const2 = lambda bi, si: (0, 0)
    tok = lambda bi, si: (bi, si, 0)
    tr = lambda bi, si: (bi, 0, si)
    n_gate = GATE_PIECES * LANES
    return pl.pallas_call(
        functools.partial(_inproj_kernel, d_model=d, ts=ts),
        grid=(b, s // ts),
        in_specs=[
            pl.BlockSpec((1, ts, d), tok),
            pl.BlockSpec((1, 1, n6), lambda bi, si: (bi, 0, 0)),
            pl.BlockSpec((1, d), const2),
            pl.BlockSpec((d, IN_PAD), const2),
            pl.BlockSpec((1, LANES), const2),
            pl.BlockSpec((1, MLA_Q_RANK), const2),
            pl.BlockSpec((1, MLA_KV_RANK), const2),
            pl.BlockSpec((MLA_Q_RANK, 2 * MLA_QK_PAD), const2),
            pl.BlockSpec((MLA_KV_RANK, MLA_QK_PAD + MLA_WIDTH), const2),
            pl.BlockSpec((ts, 4 * LANES), lambda bi, si: (si, 0)),
            pl.BlockSpec((n_gate, FOX_PAIRS * LANES), const2),
            pl.BlockSpec((n_gate, FOX_PAIRS * LANES), const2),
            pl.BlockSpec((1, FOX_PAIRS * LANES), const2),
            pl.BlockSpec((1, FOX_PAIRS * LANES), const2),
        ],
        out_specs=[
            pl.BlockSpec((1, DIFF_WIDTH, ts), tr),
            pl.BlockSpec((1, ts, DIFF_WIDTH), tok),
            pl.BlockSpec((1, DIFF_WIDTH, ts), tr),
            pl.BlockSpec((1, FOX_QK_PAD, ts), tr),
            pl.BlockSpec((1, ts, FOX_QK_PAD), tok),
            pl.BlockSpec((1, FOX_WIDTH, ts), tr),
            pl.BlockSpec((1, MLA_QK_PAD, ts), tr),
            pl.BlockSpec((1, ts, MLA_QK_PAD), tok),
            pl.BlockSpec((1, MLA_WIDTH, ts), tr),
        ],
        out_shape=[
            jax.ShapeDtypeStruct((b, DIFF_WIDTH, s), BF16),
            jax.ShapeDtypeStruct((b, s, DIFF_WIDTH), BF16),
            jax.ShapeDtypeStruct((b, DIFF_WIDTH, s), BF16),
            jax.ShapeDtypeStruct((b, FOX_QK_PAD, s), BF16),
            jax.ShapeDtypeStruct((b, s, FOX_QK_PAD), BF16),
            jax.ShapeDtypeStruct((b, FOX_WIDTH, s), BF16),
            jax.ShapeDtypeStruct((b, MLA_QK_PAD, s), BF16),
            jax.ShapeDtypeStruct((b, s, MLA_QK_PAD), BF16),
            jax.ShapeDtypeStruct((b, MLA_WIDTH, s), BF16),
        ],
        scratch_shapes=[pltpu.VMEM((8, LANES), F32)],
        compiler_params=_cparams(("parallel", "arbitrary")),
        name="norm_inproj",
    )(x, mod_l.reshape(b, 1, n6), g.reshape(1, d), win_p, b_misc, qg.reshape(1, -1), kvg.reshape(1, -1),
      wq2, wkv2, tab, jnp.asarray(sel_k, BF16), jnp.asarray(sel_q, BF16), jnp.asarray(one_k), jnp.asarray(one_q))


def _attn_step(maps, state, j, kind, t):
    m_ref, l_ref, acc_ref = state
    start = pl.multiple_of(j * t, t)

    def logits(mp):
        return jnp.dot(mp["load_k"](start), mp["load_qt"](), preferred_element_type=F32)

    s_all = [logits(mp) for mp in maps]
    for idx, mp in enumerate(maps):
        s = mp["bias"](s_all[idx], kind)
        m_prev = m_ref[idx]
        m_new = jnp.maximum(m_prev, jnp.max(s, axis=0, keepdims=True))
        alpha = jnp.exp2(m_prev - m_new)
        p = jnp.exp2(s - m_new)
        l_ref[idx] = alpha * l_ref[idx] + jnp.sum(p, axis=0, keepdims=True)
        acc_ref[idx] = alpha * acc_ref[idx] + jnp.dot(mp["load_vt"](start), p.astype(BF16),
                                                      preferred_element_type=F32)
        m_ref[idx] = m_new


def _attn_run(maps, state, qi, t, near_tile):
    m_ref, l_ref, acc_ref = state
    m_ref[...] = jnp.full(m_ref.shape, -jnp.inf, F32)
    l_ref[...] = jnp.zeros(l_ref.shape, F32)
    acc_ref[...] = jnp.zeros(acc_ref.shape, F32)

    def body(j, c):
        _attn_step(maps, state, j, "far", t)
        return c

    if near_tile:
        lax.fori_loop(0, jnp.maximum(qi - 1, 0), body, 0)

        @pl.when(qi >= 1)
        def _():
            _attn_step(maps, state, qi - 1, "near", t)
    else:
        lax.fori_loop(0, qi, body, 0)
    _attn_step(maps, state, qi, "diag", t)
    return [acc_ref[idx] / l_ref[idx] for idx in range(len(maps))]


def _causal_bias(s, kind):
    if kind != "diag":
        return s
    key = lax.broadcasted_iota(jnp.int32, s.shape, 0)
    qry = lax.broadcasted_iota(jnp.int32, s.shape, 1)
    return jnp.where(key <= qry, s, -jnp.inf)


def _attn_scratch(n_maps, dv, t):
    return [pltpu.VMEM((n_maps, 1, t), F32), pltpu.VMEM((n_maps, 1, t), F32), pltpu.VMEM((n_maps, dv, t), F32)]


def _diff_attn_kernel(qt_ref, k_ref, vt_ref, bias_ref, lam_ref, g_ref, o_ref, qm_ref, m_ref, l_ref, acc_ref,
                      *, t, lam_init):
    qi = pl.program_id(1)
    row = lax.broadcasted_iota(jnp.int32, (LANES, t), 0)
    lp = lam_ref[...]
    lam = (jnp.exp(jnp.sum(lp[0:1] * lp[1:2], axis=1, keepdims=True))
           - jnp.exp(jnp.sum(lp[2:3] * lp[3:4], axis=1, keepdims=True)) + lam_init)
    maps = []
    for h in range(DIFF_HEADS):
        p, i = divmod(h, 2)
        qt_pair = qt_ref[0, p * LANES:(p + 1) * LANES, :]

        def bias(s, kind, h=h):
            if kind == "far":
                return s
            return bias_ref[h, 0 if kind == "diag" else 1] + s

        load_k = lambda start, p=p: k_ref[0, pl.ds(start, t), p * LANES:(p + 1) * LANES]
        load_vt = lambda start, h=h: vt_ref[0, h * DIFF_V_DIM:(h + 1) * DIFF_V_DIM, pl.ds(start, t)]
        for mp in range(2):
            idx = 2 * h + mp
            lo = (2 * i + mp) * DIFF_QK_DIM
            qm_ref[idx] = jnp.where((row >= lo) & (row < lo + DIFF_QK_DIM), qt_pair, jnp.zeros_like(qt_pair))
            maps.append(dict(load_qt=lambda idx=idx: qm_ref[idx], load_k=load_k, load_vt=load_vt, bias=bias))
    outs = _attn_run(maps, (m_ref, l_ref, acc_ref), qi, t, near_tile=True)
    heads = []
    for h in range(DIFF_HEADS):
        o = outs[2 * h] - lam * outs[2 * h + 1]
        ms = jnp.mean(o * o, axis=0, keepdims=True)
        heads.append(o * lax.rsqrt(ms + NORM_EPS))
    ot = jnp.concatenate(heads, axis=0)
    o_ref[0] = (ot.T * g_ref[...] * (1.0 - lam_init)).astype(o_ref.dtype)


def _diff_attention(qt, k, vt, bias_tiles, diff_lambda, subln_g, lam_init, t):
    b, s, _ = k.shape
    n_maps = 2 * DIFF_HEADS
    g_all = jnp.tile(subln_g, DIFF_HEADS).reshape(1, DIFF_WIDTH)
    return pl.pallas_call(
        functools.partial(_diff_attn_kernel, t=t, lam_init=lam_init),
        grid=(b, s // t),
        in_specs=[
            pl.BlockSpec((1, DIFF_WIDTH, t), lambda bi, qi: (bi, 0, qi)),
            pl.BlockSpec((1, s, DIFF_WIDTH), lambda bi, qi: (bi, 0, 0)),
            pl.BlockSpec((1, DIFF_WIDTH, s), lambda bi, qi: (bi, 0, 0)),
            pl.BlockSpec((DIFF_HEADS, 2, t, t), lambda bi, qi: (0, 0, 0, 0)),
            pl.BlockSpec((4, DIFF_QK_DIM), lambda bi, qi: (0, 0)),
            pl.BlockSpec((1, DIFF_WIDTH), lambda bi, qi: (0, 0)),
        ],
        out_specs=pl.BlockSpec((1, t, DIFF_WIDTH), lambda bi, qi: (bi, qi, 0)),
        out_shape=jax.ShapeDtypeStruct((b, s, DIFF_WIDTH), BF16),
        scratch_shapes=[pltpu.VMEM((n_maps, LANES, t), BF16)] + _attn_scratch(n_maps, DIFF_V_DIM, t),
        compiler_params=_cparams(("parallel", "arbitrary")),
        name="diff_attention",
    )(qt, k, vt, bias_tiles, diff_lambda, g_all)


def _fox_attn_kernel(qt_ref, k_ref, vt_ref, o_ref, qm_ref, m_ref, l_ref, acc_ref, *, t):
    qi = pl.program_id(1)
    row = lax.broadcasted_iota(jnp.int32, (2 * LANES, t), 0)
    maps = []
    for h in range(FOX_HEADS):
        p, i = divmod(h, 2)
        blk = 2 * p * LANES
        qt_pair = qt_ref[0, blk:blk + 2 * LANES, :]
        lo = i * FOX_HEAD_DIM
        glo = LANES + 2 * GATE_PIECES * i
        own = ((row >= lo) & (row < lo + FOX_HEAD_DIM)) | ((row >= glo) & (row < glo + 2 * GATE_PIECES))
        qm_ref[h] = jnp.where(own, qt_pair, jnp.zeros_like(qt_pair))
        load_k = lambda start, blk=blk: k_ref[0, pl.ds(start, t), blk:blk + 2 * LANES]
        load_vt = lambda start, h=h: vt_ref[0, h * FOX_HEAD_DIM:(h + 1) * FOX_HEAD_DIM, pl.ds(start, t)]
        maps.append(dict(load_qt=lambda h=h: qm_ref[h], load_k=load_k, load_vt=load_vt, bias=_causal_bias))
    outs = _attn_run(maps, (m_ref, l_ref, acc_ref), qi, t, near_tile=False)
    o_ref[0] = jnp.concatenate(outs, axis=0).T.astype(o_ref.dtype)


def _fox_attention(qt, k, vt, t):
    b, s, _ = k.shape
    return pl.pallas_call(
        functools.partial(_fox_attn_kernel, t=t),
        grid=(b, s // t),
        in_specs=[
            pl.BlockSpec((1, FOX_QK_PAD, t), lambda bi, qi: (bi, 0, qi)),
            pl.BlockSpec((1, s, FOX_QK_PAD), lambda bi, qi: (bi, 0, 0)),
            pl.BlockSpec((1, FOX_WIDTH, s), lambda bi, qi: (bi, 0, 0)),
        ],
        out_specs=pl.BlockSpec((1, t, FOX_WIDTH), lambda bi, qi: (bi, qi, 0)),
        out_shape=jax.ShapeDtypeStruct((b, s, FOX_WIDTH), BF16),
        scratch_shapes=[pltpu.VMEM((FOX_HEADS, 2 * LANES, t), BF16)] + _attn_scratch(FOX_HEADS, FOX_HEAD_DIM, t),
        compiler_params=_cparams(("parallel", "arbitrary")),
        name="fox_attention",
    )(qt, k, vt)


def _mla_attn_kernel(qt_ref, k_ref, vt_ref, o_ref, m_ref, l_ref, acc_ref, *, t):
    qi = pl.program_id(1)
    maps = []
    for h in range(MLA_HEADS):
        load_qt = lambda h=h: qt_ref[0, h * LANES:(h + 1) * LANES, :]
        load_k = lambda start, h=h: k_ref[0, pl.ds(start, t), h * LANES:(h + 1) * LANES]
        load_vt = lambda start, h=h: vt_ref[0, h * MLA_V_DIM:(h + 1) * MLA_V_DIM, pl.ds(start, t)]
        maps.append(dict(load_qt=load_qt, load_k=load_k, load_vt=load_vt, bias=_causal_bias))
    outs = _attn_run(maps, (m_ref, l_ref, acc_ref), qi, t, near_tile=False)
    o_ref[0] = jnp.concatenate(outs, axis=0).T.astype(o_ref.dtype)


def _mla_attention(qt, k, vt, t):
    b, s, _ = k.shape
    return pl.pallas_call(
        functools.partial(_mla_attn_kernel, t=t),
        grid=(b, s // t),
        in_specs=[
            pl.BlockSpec((1, MLA_QK_PAD, t), lambda bi, qi: (bi, 0, qi)),
            pl.BlockSpec((1, s, MLA_QK_PAD), lambda bi, qi: (bi, 0, 0)),
            pl.BlockSpec((1, MLA_WIDTH, s), lambda bi, qi: (bi, 0, 0)),
        ],
        out_specs=pl.BlockSpec((1, t, MLA_WIDTH), lambda bi, qi: (bi, qi, 0)),
        out_shape=jax.ShapeDtypeStruct((b, s, MLA_WIDTH), BF16),
        scratch_shapes=_attn_scratch(MLA_HEADS, MLA_V_DIM, t),
        compiler_params=_cparams(("parallel", "arbitrary")),
        name="mla_attention",
    )(qt, k, vt)


def _outproj_kernel(*refs, d_model, with_router):
    if with_router:
        od_ref, of_ref, om_ref, w_ref, x_ref, mod_ref, g_ref, rw_ref, xo_ref, h_ref, cmb_ref = refs
    else:
        od_ref, of_ref, om_ref, w_ref, x_ref, mod_ref, g_ref, xo_ref, h_ref = refs
    d = d_model
    mix = jnp.dot(od_ref[0], w_ref[0:DIFF_WIDTH, :], preferred_element_type=F32)
    mix += jnp.dot(of_ref[0], w_ref[DIFF_WIDTH:DIFF_WIDTH + FOX_WIDTH, :], preferred_element_type=F32)
    mix += jnp.dot(om_ref[0], w_ref[DIFF_WIDTH + FOX_WIDTH:, :], preferred_element_type=F32)
    g_a = mod_ref[0, :, 2 * d:3 * d]
    sh_f = mod_ref[0, :, 3 * d:4 * d]
    sc_f = mod_ref[0, :, 4 * d:5 * d]
    xn = x_ref[0] + g_a * mix
    xo_ref[0] = xn
    h = _rms(xn, g_ref[...]) * (1.0 + sc_f) + sh_f
    h_ref[0] = h.astype(BF16)
    if with_router:
        logits = jnp.dot(h, rw_ref[...], preferred_element_type=F32, precision=lax.Precision.HIGHEST)
        lane = lax.broadcasted_iota(jnp.int32, logits.shape, 1).astype(F32)
        neg = -jnp.inf
        lg = jnp.where(lane < N_EXPERTS, logits, neg)
        m1 = jnp.max(lg, axis=1, keepdims=True)
        i1 = jnp.min(jnp.where(lg == m1, lane, float(LANES)), axis=1, keepdims=True)
        lg2 = jnp.where(lane == i1, neg, lg)
        m2 = jnp.max(lg2, axis=1, keepdims=True)
        i2 = jnp.min(jnp.where(lg2 == m2, lane, float(LANES)), axis=1, keepdims=True)
        e2 = jnp.exp(m2 - m1)
        w1 = 1.0 / (1.0 + e2)
        w2 = e2 / (1.0 + e2)
        cmb_ref[0] = jnp.where(lane == i1, w1, 0.0) + jnp.where(lane == i2, w2, 0.0)


def _outproj(od, of, om, w_out_b, x, mod_l, g, router_pad, ts):
    b, s, d = x.shape
    n6 = mod_l.shape[-1]
    with_router = router_pad is not None
    tok = lambda bi, si: (bi, si, 0)
    const2 = lambda bi, si: (0, 0)
    in_specs = [
        pl.BlockSpec((1, ts, DIFF_WIDTH), tok),
        pl.BlockSpec((1, ts, FOX_WIDTH), tok),
        pl.BlockSpec((1, ts, MLA_WIDTH), tok),
        pl.BlockSpec((d, d), const2),
        pl.BlockSpec((1, ts, d), tok),
        pl.BlockSpec((1, 1, n6), lambda bi, si: (bi, 0, 0)),
        pl.BlockSpec((1, d), const2),
    ]
    args = [od, of, om, w_out_b, x, mod_l.reshape(b, 1, n6), g.reshape(1, d)]
    out_specs = [pl.BlockSpec((1, ts, d), tok), pl.BlockSpec((1, ts, d), tok)]
    out_shape = [jax.ShapeDtypeStruct((b, s, d), F32), jax.ShapeDtypeStruct((b, s, d), BF16)]
    if with_router:
        in_specs.append(pl.BlockSpec((d, LANES), const2))
        args.append(router_pad)
        out_specs.append(pl.BlockSpec((1, ts, LANES), tok))
        out_shape.append(jax.ShapeDtypeStruct((b, s, LANES), F32))
    return pl.pallas_call(
        functools.partial(_outproj_kernel, d_model=d, with_router=with_router),
        grid=(b, s // ts),
        in_specs=in_specs,
        out_specs=out_specs,
        out_shape=out_shape,
        compiler_params=_cparams(("parallel", "parallel")),
        name="outproj_router" if with_router else "outproj",
    )(*args)


def _finish(x_ref, mod_ref, acc, fg_ref, o_ref, d, final):
    g_f = mod_ref[0, :, 5 * d:6 * d]
    xo = x_ref[0] + g_f * acc
    if final:
        xo = _rms(xo, fg_ref[...])
    o_ref[0] = xo


def _ffn_kernel(h_ref, wg_ref, wu_ref, wd_ref, x_ref, mod_ref, fg_ref, o_ref, acc_ref, *, d_model, final):
    j = pl.program_id(2)

    @pl.when(j == 0)
    def _():
        acc_ref[...] = jnp.zeros_like(acc_ref)

    h = h_ref[0]
    a = jnp.dot(h, wg_ref[...], preferred_element_type=F32)
    u = jnp.dot(h, wu_ref[...], preferred_element_type=F32)
    act = (_silu(a) * u).astype(BF16)
    acc_ref[...] += jnp.dot(act, wd_ref[...], preferred_element_type=F32)

    @pl.when(j == pl.num_programs(2) - 1)
    def _():
        _finish(x_ref, mod_ref, acc_ref[...], fg_ref, o_ref, d_model, final)


def _ffn_dense(h, wg, wu, wd, x, mod_l, final_g, final, tm, tf):
    b, s, d = x.shape
    n6 = mod_l.shape[-1]
    dff = wg.shape[1]
    tok = lambda bi, si, j: (bi, si, 0)
    return pl.pallas_call(
        functools.partial(_ffn_kernel, d_model=d, final=final),
        grid=(b, s // tm, dff // tf),
        in_specs=[
            pl.BlockSpec((1, tm, d), tok),
            pl.BlockSpec((d, tf), lambda bi, si, j: (0, j)),
            pl.BlockSpec((d, tf), lambda bi, si, j: (0, j)),
            pl.BlockSpec((tf, d), lambda bi, si, j: (j, 0)),
            pl.BlockSpec((1, tm, d), tok),
            pl.BlockSpec((1, 1, n6), lambda bi, si, j: (bi, 0, 0)),
            pl.BlockSpec((1, d), lambda bi, si, j: (0, 0)),
        ],
        out_specs=pl.BlockSpec((1, tm, d), tok),
        out_shape=jax.ShapeDtypeStruct((b, s, d), F32),
        scratch_shapes=[pltpu.VMEM((tm, d), F32)],
        compiler_params=_cparams(("parallel", "parallel", "arbitrary")),
        name="ffn_dense",
    )(h, wg, wu, wd, x, mod_l.reshape(b, 1, n6), final_g.reshape(1, d))


def _moe_kernel(h_ref, cmb_ref, wg_ref, wu_ref, wd_ref, x_ref, mod_ref, fg_ref, o_ref, acc_ref, *, d_model, final):
    e = pl.program_id(2)
    j = pl.program_id(3)

    @pl.when((e == 0) & (j == 0))
    def _():
        acc_ref[...] = jnp.zeros_like(acc_ref)

    h = h_ref[0]
    a = jnp.dot(h, wg_ref[0], preferred_element_type=F32)
    u = jnp.dot(h, wu_ref[0], preferred_element_type=F32)
    act = (_silu(a) * u).astype(BF16)
    lane = lax.broadcasted_iota(jnp.int32, (1, LANES), 1)
    ce = jnp.sum(jnp.where(lane == e, cmb_ref[0], 0.0), axis=1, keepdims=True)
    acc_ref[...] += ce * jnp.dot(act, wd_ref[0], preferred_element_type=F32)

    @pl.when((e == pl.num_programs(2) - 1) & (j == pl.num_programs(3) - 1))
    def _():
        _finish(x_ref, mod_ref, acc_ref[...], fg_ref, o_ref, d_model, final)


def _moe_dense(h, cmb, wg, wu, wd, x, mod_l, final_g, final, tm, tf):
    b, s, d = x.shape
    n6 = mod_l.shape[-1]
    ne, _, dff = wg.shape
    tok = lambda bi, si, e, j: (bi, si, 0)
    return pl.pallas_call(
        functools.partial(_moe_kernel, d_model=d, final=final),
        grid=(b, s // tm, ne, dff // tf),
        in_specs=[
            pl.BlockSpec((1, tm, d), tok),
            pl.BlockSpec((1, tm, LANES), tok),
            pl.BlockSpec((1, d, tf), lambda bi, si, e, j: (e, 0, j)),
            pl.BlockSpec((1, d, tf), lambda bi, si, e, j: (e, 0, j)),
            pl.BlockSpec((1, tf, d), lambda bi, si, e, j: (e, j, 0)),
            pl.BlockSpec((1, tm, d), tok),
            pl.BlockSpec((1, 1, n6), lambda bi, si, e, j: (bi, 0, 0)),
            pl.BlockSpec((1, d), lambda bi, si, e, j: (0, 0)),
        ],
        out_specs=pl.BlockSpec((1, tm, d), tok),
        out_shape=jax.ShapeDtypeStruct((b, s, d), F32),
        scratch_shapes=[pltpu.VMEM((tm, d), F32)],
        compiler_params=_cparams(("parallel", "parallel", "arbitrary", "arbitrary")),
        name="moe_dense",
    )(h, cmb, wg, wu, wd, x, mod_l.reshape(b, 1, n6), final_g.reshape(1, d))


def _prep_w_in(w_in_l):
    d = w_in_l.shape[0]
    sizes = [DIFF_WIDTH, DIFF_WIDTH, DIFF_WIDTH, FOX_WIDTH, FOX_WIDTH, FOX_WIDTH, FOX_HEADS,
             MLA_Q_RANK, MLA_KV_RANK, MLA_ROPE_DIM]
    offs = np.cumsum([0] + sizes)
    sec = [w_in_l[:, offs[i]:offs[i + 1]] for i in range(len(sizes))]
    dq, dk, dv, fq, fk, fv, ff, cq, ckv, kr = sec
    half = MLA_ROPE_DIM // 2
    z = lambda n: jnp.zeros((d, n), w_in_l.dtype)
    kr_sw = jnp.concatenate([kr[:, half:], kr[:, :half]], axis=1)
    pad_r = LANES - MLA_NOPE_DIM - MLA_ROPE_DIM
    cols = [dq, dk, dv, fq, fk, fv, cq, ckv,
            z(MLA_NOPE_DIM), kr, z(pad_r), z(MLA_NOPE_DIM), kr_sw, z(pad_r),
            ff, z(LANES - FOX_HEADS)]
    return jnp.concatenate(cols, axis=1).astype(BF16)


def _prep_w_uq(w_uq_l):
    r = w_uq_l.shape[0]
    w = w_uq_l.reshape(r, MLA_HEADS, MLA_NOPE_DIM + MLA_ROPE_DIM)
    nope, rope = w[..., :MLA_NOPE_DIM], w[..., MLA_NOPE_DIM:]
    half = MLA_ROPE_DIM // 2
    rope_sw = jnp.concatenate([rope[..., half:], rope[..., :half]], axis=-1)
    pad_r = LANES - MLA_NOPE_DIM - MLA_ROPE_DIM
    zr = jnp.zeros((r, MLA_HEADS, pad_r), w.dtype)
    zn = jnp.zeros((r, MLA_HEADS, MLA_NOPE_DIM), w.dtype)
    plain = jnp.concatenate([nope, rope, zr], axis=-1).reshape(r, MLA_QK_PAD)
    swapped = jnp.concatenate([zn, rope_sw, zr], axis=-1).reshape(r, MLA_QK_PAD)
    return jnp.concatenate([plain, swapped], axis=1).astype(BF16)


def _prep_w_ukv(w_ukv_l):
    r = w_ukv_l.shape[0]
    w = w_ukv_l.reshape(r, MLA_HEADS, MLA_NOPE_DIM + MLA_V_DIM)
    k_nope, v = w[..., :MLA_NOPE_DIM], w[..., MLA_NOPE_DIM:]
    zk = jnp.zeros((r, MLA_HEADS, LANES - MLA_NOPE_DIM), w.dtype)
    k_pad = jnp.concatenate([k_nope, zk], axis=-1).reshape(r, MLA_QK_PAD)
    return jnp.concatenate([k_pad, v.reshape(r, MLA_WIDTH)], axis=1).astype(BF16)


def _rope_tables(seq):
    half = MLA_ROPE_DIM // 2
    freqs = ROPE_THETA ** (-jnp.arange(half, dtype=F32) / half)
    ang = jnp.arange(seq, dtype=F32)[:, None] * freqs[None, :]
    cos, sin = jnp.cos(ang), jnp.sin(ang)
    scale = (MLA_NOPE_DIM + MLA_ROPE_DIM) ** -0.5 * LOG2E
    pad_r = LANES - MLA_NOPE_DIM - MLA_ROPE_DIM
    ones = jnp.ones((seq, MLA_NOPE_DIM), F32)
    zn = jnp.zeros((seq, MLA_NOPE_DIM), F32)
    zr = jnp.zeros((seq, pad_r), F32)
    c_q = jnp.concatenate([ones, cos, cos, zr], axis=1) * scale
    s_q = jnp.concatenate([zn, -sin, sin, zr], axis=1) * scale
    c_k = jnp.concatenate([zn, cos, cos, zr], axis=1)
    s_k = jnp.concatenate([zn, -sin, sin, zr], axis=1)
    return jnp.concatenate([c_q, s_q, c_k, s_k], axis=1)


def _lambda_init(layer_idx):
    return 0.8 - 0.6 * math.exp(-0.3 * layer_idx)


def _pick(n, cands):
    for c in cands:
        if n % c == 0:
            return c
    return n


def kernel(x, c, w_ada, b_ada, attn_norm, ffn_norm, w_in, b_forget, diff_lambda, diff_subln, rel_bias, mla_q_norm,
           mla_kv_norm, w_uq, w_ukv, w_out, ffn_w_gate, ffn_w_up, ffn_w_down, router_w, moe_w_gate, moe_w_up,
           moe_w_down, final_norm):
    b, s, d = x.shape
    depth = w_ada.shape[0]
    t = ATTN_T
    ts_a = _pick(s, (512, 256))
    ts_w = _pick(s, (512, 256))
    tm = _pick(s, (512, 256))

    mod = _modulation(c, w_ada, b_ada)
    bias_tiles = _bias_tiles(rel_bias, t)
    tab = _rope_tables(s)

    for l in range(depth):
        mod_l = mod[l]
        win_p = _prep_w_in(w_in[l])
        b_misc = jnp.concatenate([b_forget[l], jnp.zeros((LANES - FOX_HEADS,), F32)]).reshape(1, LANES)
        qtd, kd, vtd, qtf, kf, vtf, qtm, km, vtm = _inproj(
            x, mod_l, attn_norm[l], win_p, b_misc, mla_q_norm[l], mla_kv_norm[l],
            _prep_w_uq(w_uq[l]), _prep_w_ukv(w_ukv[l]), tab, ts_a)
        od = _diff_attention(qtd, kd, vtd, bias_tiles, diff_lambda[l], diff_subln[l], _lambda_init(l), t)
        of = _fox_attention(qtf, kf, vtf, t)
        om = _mla_attention(qtm, km, vtm, t)

        is_moe = l % 2 == 1
        i = l // 2
        final = l == depth - 1
        if is_moe:
            rpad = jnp.concatenate([router_w[i], jnp.zeros((d, LANES - N_EXPERTS), F32)], axis=1)
            x, h2, cmb = _outproj(od, of, om, w_out[l].astype(BF16), x, mod_l, ffn_norm[l], rpad, ts_w)
            dff = moe_w_gate.shape[-1]
            x = _moe_dense(h2, cmb, moe_w_gate[i].astype(BF16), moe_w_up[i].astype(BF16),
                           moe_w_down[i].astype(BF16), x, mod_l, final_norm, final, tm, _pick(dff, (896, 512, 256)))
        else:
            x, h2 = _outproj(od, of, om, w_out[l].astype(BF16), x, mod_l, ffn_norm[l], None, ts_w)
            dff = ffn_w_gate.shape[-1]
            x = _ffn_dense(h2, ffn_w_gate[i].astype(BF16), ffn_w_up[i].astype(BF16), ffn_w_down[i].astype(BF16),
                           x, mod_l, final_norm, final, tm, _pick(dff, (1408, 512, 256)))
    return x
```

```python
import functools
import math

import numpy as np
import jax
import jax.numpy as jnp
from jax import lax
from jax.experimental import pallas as pl
from jax.experimental.pallas import tpu as pltpu

F32 = jnp.float32
BF16 = jnp.bfloat16

DIFF_HEADS = 4
DIFF_QK_DIM = 32
DIFF_V_DIM = 64
FOX_HEADS = 6
FOX_HEAD_DIM = 64
MLA_HEADS = 6
MLA_Q_RANK = 256
MLA_KV_RANK = 128
MLA_NOPE_DIM = 64
MLA_ROPE_DIM = 32
MLA_V_DIM = 64
ROPE_THETA = 10000.0
REL_BUCKETS = 32
REL_MAX_DIST = 128
DIFF_WIDTH = DIFF_HEADS * DIFF_V_DIM
FOX_WIDTH = FOX_HEADS * FOX_HEAD_DIM
MLA_WIDTH = MLA_HEADS * MLA_V_DIM
N_EXPERTS = 8
NORM_EPS = 1e-6
LOG2E = math.log2(math.e)

LANES = 128
VMEM_LIMIT_BYTES = 56 * 1024 * 1024

OFF_D = 0
OFF_F = OFF_D + 3 * DIFF_WIDTH
OFF_CQ = OFF_F + 3 * FOX_WIDTH
OFF_CKV = OFF_CQ + MLA_Q_RANK
OFF_KR = OFF_CKV + MLA_KV_RANK
OFF_KRS = OFF_KR + LANES
OFF_MISC = OFF_KRS + LANES
IN_PAD = OFF_MISC + LANES
MLA_QK_PAD = MLA_HEADS * LANES
FOX_PAIRS = FOX_HEADS // 2
FOX_QK_PAD = FOX_PAIRS * 2 * LANES
GATE_PIECES = 3

ATTN_T = 256
MOE_TILE = 512


def _cparams(sem):
    return pltpu.CompilerParams(dimension_semantics=sem, vmem_limit_bytes=VMEM_LIMIT_BYTES)


def _sigmoid(x):
    return 1.0 / (1.0 + jnp.exp(-x))


def _silu(x):
    return x * _sigmoid(x)


def _bucket_thresholds():
    max_exact = REL_BUCKETS // 2
    n = np.arange(0, 4096)
    nf = np.maximum(n, 1).astype(np.float32)
    large = max_exact + (np.log(nf / np.float32(max_exact)) / np.float32(math.log(REL_MAX_DIST / max_exact))
                         * np.float32(REL_BUCKETS - max_exact)).astype(np.int32)
    large = np.minimum(large, REL_BUCKETS - 1)
    bucket = np.where(n < max_exact, n, large)
    assert np.all(np.diff(bucket) >= 0)
    thr = [int(np.argmax(bucket >= k)) for k in range(REL_BUCKETS)]
    assert all(bucket[thr[k]] >= k for k in range(REL_BUCKETS))
    return thr, bucket


def _mod_kernel(c_ref, w_ref, b_ref, o_ref):
    ca = _silu(c_ref[...]).astype(BF16)
    w = w_ref[0].astype(BF16)
    o_ref[0] = jnp.dot(ca, w, preferred_element_type=F32) + b_ref[0]


def _modulation(c, w_ada, b_ada):
    depth, d, n6 = w_ada.shape
    b = c.shape[0]
    tn = n6 // 4
    return pl.pallas_call(
        _mod_kernel,
        grid=(depth, n6 // tn),
        in_specs=[
            pl.BlockSpec((b, d), lambda l, j: (0, 0)),
            pl.BlockSpec((1, d, tn), lambda l, j: (l, 0, j)),
            pl.BlockSpec((1, 1, tn), lambda l, j: (l, 0, j)),
        ],
        out_specs=pl.BlockSpec((1, b, tn), lambda l, j: (l, 0, j)),
        out_shape=jax.ShapeDtypeStruct((depth, b, n6), F32),
        compiler_params=_cparams(("parallel", "parallel")),
        name="adaln_mod",
    )(c, w_ada, b_ada.reshape(depth, 1, n6))


def _bias_tile_kernel(rb_ref, o_ref, *, thr, t, far_bucket):
    h = pl.program_id(0)
    delta = pl.program_id(1)
    key = lax.broadcasted_iota(jnp.int32, (t, t), 0)
    qry = lax.broadcasted_iota(jnp.int32, (t, t), 1)
    n = delta * t + qry - key
    val = jnp.full((t, t), rb_ref[0, h], F32)
    for k in range(1, REL_BUCKETS):
        val = jnp.where(n >= thr[k], rb_ref[k, h], val)
    val = (val - rb_ref[far_bucket, h]) * LOG2E
    o_ref[0, 0] = jnp.where(n >= 0, val, -jnp.inf)


def _bias_tiles(rel_bias, t):
    thr, bucket = _bucket_thresholds()
    far_bucket = int(bucket[t + 1])
    assert np.all(bucket[t + 1:] == far_bucket)
    return pl.pallas_call(
        functools.partial(_bias_tile_kernel, thr=thr, t=t, far_bucket=far_bucket),
        grid=(DIFF_HEADS, 2),
        in_specs=[pl.BlockSpec(memory_space=pltpu.SMEM)],
        out_specs=pl.BlockSpec((1, 1, t, t), lambda h, d: (h, d, 0, 0)),
        out_shape=jax.ShapeDtypeStruct((DIFF_HEADS, 2, t, t), F32),
        compiler_params=_cparams(("parallel", "parallel")),
        name="rel_bias_tiles",
    )(rel_bias)


def _rms(x, g):
    return x * lax.rsqrt(jnp.mean(x * x, axis=-1, keepdims=True) + NORM_EPS) * g


def _gate_select_constants():
    n_in = GATE_PIECES * LANES
    n_out = FOX_PAIRS * LANES
    sel_k = np.zeros((n_in, n_out), np.float32)
    sel_q = np.zeros((n_in, n_out), np.float32)
    one_k = np.zeros((1, n_out), np.float32)
    one_q = np.zeros((1, n_out), np.float32)
    for h in range(FOX_HEADS):
        p, i = divmod(h, 2)
        base = p * LANES + 2 * GATE_PIECES * i
        for c in range(GATE_PIECES):
            sel_k[c * LANES + h, base + c] = -1.0
            one_k[0, base + GATE_PIECES + c] = 1.0
            one_q[0, base + c] = 1.0
            sel_q[c * LANES + h, base + GATE_PIECES + c] = 1.0
    return sel_k, sel_q, one_k, one_q


def _inproj_kernel(x_ref, mod_ref, g_ref, win_ref, bm_ref, qg_ref, kvg_ref, wq2_ref, wkv2_ref, tab_ref,
                   selk_ref, selq_ref, onek_ref, oneq_ref,
                   qtd_ref, kd_ref, vtd_ref, qtf_ref, kf_ref, vtf_ref, qtm_ref, km_ref, vtm_ref, carry_ref,
                   *, d_model, ts):
    s_idx = pl.program_id(1)
    x = x_ref[0]
    sh = mod_ref[0, :, 0:d_model]
    sc = mod_ref[0, :, d_model:2 * d_model]
    h = _rms(x, g_ref[...]) * (1.0 + sc) + sh
    hb = h.astype(BF16)

    def proj(lo, hi):
        return jnp.dot(hb, win_ref[:, lo:hi], preferred_element_type=F32)

    pd = proj(OFF_D, OFF_F)
    qtd_ref[0] = (pd[:, 0:DIFF_WIDTH] * (DIFF_QK_DIM ** -0.5 * LOG2E)).T.astype(BF16)
    kd_ref[0] = pd[:, DIFF_WIDTH:2 * DIFF_WIDTH].astype(BF16)
    vtd_ref[0] = pd[:, 2 * DIFF_WIDTH:].T.astype(BF16)

    @pl.when(s_idx == 0)
    def _():
        carry_ref[...] = jnp.zeros_like(carry_ref)

    z = proj(OFF_MISC, IN_PAD) + bm_ref[...]
    logf = jnp.minimum(z, 0.0) - jnp.log(1.0 + jnp.exp(-jnp.abs(z)))
    row = lax.broadcasted_iota(jnp.int32, (ts, ts), 0)
    col = lax.broadcasted_iota(jnp.int32, (ts, ts), 1)
    tri = jnp.where(row >= col, 1.0, 0.0).astype(F32)
    cum = jnp.dot(tri, logf, preferred_element_type=F32, precision=lax.Precision.HIGHEST) + carry_ref[0:1, :]
    carry_ref[...] = jnp.broadcast_to(cum[ts - 1:ts, :], carry_ref.shape)
    rem = cum * LOG2E
    pieces = []
    for _ in range(GATE_PIECES):
        piece = rem.astype(BF16)
        pieces.append(piece)
        rem = rem - piece.astype(F32)
    gp = jnp.concatenate(pieces, axis=1)
    gate_k = jnp.dot(gp, selk_ref[...], preferred_element_type=F32) + onek_ref[...]
    gate_q = jnp.dot(gp, selq_ref[...], preferred_element_type=F32) + oneq_ref[...]

    pf = proj(OFF_F, OFF_CQ)
    for p in range(FOX_PAIRS):
        lo, hi = p * LANES, (p + 1) * LANES
        blk = 2 * p * LANES
        qtf_ref[0, blk:blk + LANES, :] = (pf[:, lo:hi] * (FOX_HEAD_DIM ** -0.5 * LOG2E)).T.astype(BF16)
        qtf_ref[0, blk + LANES:blk + 2 * LANES, :] = gate_q[:, lo:hi].T.astype(BF16)
        kf_ref[0, :, blk:blk + LANES] = pf[:, FOX_WIDTH + lo:FOX_WIDTH + hi].astype(BF16)
        kf_ref[0, :, blk + LANES:blk + 2 * LANES] = gate_k[:, lo:hi].astype(BF16)
    vtf_ref[0] = pf[:, 2 * FOX_WIDTH:].T.astype(BF16)

    cq = _rms(proj(OFF_CQ, OFF_CKV), qg_ref[...]).astype(BF16)
    qa = jnp.dot(cq, wq2_ref[...], preferred_element_type=F32)
    ckv = _rms(proj(OFF_CKV, OFF_KR), kvg_ref[...]).astype(BF16)
    kva = jnp.dot(ckv, wkv2_ref[...], preferred_element_type=F32)
    kr = proj(OFF_KR, OFF_MISC)
    cq_t = tab_ref[:, 0:LANES]
    sq_t = tab_ref[:, LANES:2 * LANES]
    ck_t = tab_ref[:, 2 * LANES:3 * LANES]
    sk_t = tab_ref[:, 3 * LANES:4 * LANES]
    krope = kr[:, 0:LANES] * ck_t + kr[:, LANES:2 * LANES] * sk_t
    for hh in range(MLA_HEADS):
        lo, hi = hh * LANES, (hh + 1) * LANES
        qh = qa[:, lo:hi] * cq_t + qa[:, MLA_QK_PAD + lo:MLA_QK_PAD + hi] * sq_t
        qtm_ref[0, lo:hi, :] = qh.T.astype(BF16)
        km_ref[0, :, lo:hi] = (kva[:, lo:hi] + krope).astype(BF16)
    vtm_ref[0] = kva[:, MLA_QK_PAD:].T.astype(BF16)


def _inproj(x, mod_l, g, win_p, b_misc, qg, kvg, wq2, wkv2, tab, ts):
    b, s, d = x.shape
    n6 = mod_l.shape[-1]
    sel_k, sel_q, one_k, one_q = _gate_select_constants()
    const2 = lambda bi, si: (0, 0)
    tok = lambda bi, si: (bi, si, 0)
    tr = lambda bi, si: (bi, 0, si)
    n_gate = GATE_PIECES * LANES
    return pl.pallas_call(
        functools.partial(_inproj_kernel, d_model=d, ts=ts),
        grid=(b, s // ts),
        in_specs=[
            pl.BlockSpec((1, ts, d), tok),
            pl.BlockSpec((1, 1, n6), lambda bi, si: (bi, 0, 0)),
            pl.BlockSpec((1, d), const2),
            pl.BlockSpec((d, IN_PAD), const2),
            pl.BlockSpec((1, LANES), const2),
            pl.BlockSpec((1, MLA_Q_RANK), const2),
            pl.BlockSpec((1, MLA_KV_RANK), const2),
            pl.BlockSpec((MLA_Q_RANK, 2 * MLA_QK_PAD), const2),
            pl.BlockSpec((MLA_KV_RANK, MLA_QK_PAD + MLA_WIDTH), const2),
            pl.BlockSpec((ts, 4 * LANES), lambda bi, si: (si, 0)),
            pl.BlockSpec((n_gate, FOX_PAIRS * LANES), const2),
            pl.BlockSpec((n_gate, FOX_PAIRS * LANES), const2),
            pl.BlockSpec((1, FOX_PAIRS * LANES), const2),
            pl.BlockSpec((1, FOX_PAIRS * LANES), const2),
        ],
        out_specs=[
            pl.BlockSpec((1, DIFF_WIDTH, ts), tr),
            pl.BlockSpec((1, ts, DIFF_WIDTH), tok),
            pl.BlockSpec((1, DIFF_WIDTH, ts), tr),
            pl.BlockSpec((1, FOX_QK_PAD, ts), tr),
            pl.BlockSpec((1, ts, FOX_QK_PAD), tok),
            pl.BlockSpec((1, FOX_WIDTH, ts), tr),
            pl.BlockSpec((1, MLA_QK_PAD, ts), tr),
            pl.BlockSpec((1, ts, MLA_QK_PAD), tok),
            pl.BlockSpec((1, MLA_WIDTH, ts), tr),
        ],
        out_shape=[
            jax.ShapeDtypeStruct((b, DIFF_WIDTH, s), BF16),
            jax.ShapeDtypeStruct((b, s, DIFF_WIDTH), BF16),
            jax.ShapeDtypeStruct((b, DIFF_WIDTH, s), BF16),
            jax.ShapeDtypeStruct((b, FOX_QK_PAD, s), BF16),
            jax.ShapeDtypeStruct((b, s, FOX_QK_PAD), BF16),
            jax.ShapeDtypeStruct((b, FOX_WIDTH, s), BF16),
            jax.ShapeDtypeStruct((b, MLA_QK_PAD, s), BF16),
            jax.ShapeDtypeStruct((b, s, MLA_QK_PAD), BF16),
            jax.ShapeDtypeStruct((b, MLA_WIDTH, s), BF16),
        ],
        scratch_shapes=[pltpu.VMEM((8, LANES), F32)],
        compiler_params=_cparams(("parallel", "arbitrary")),
        name="norm_inproj",
    )(x, mod_l.reshape(b, 1, n6), g.reshape(1, d), win_p, b_misc, qg.reshape(1, -1), kvg.reshape(1, -1),
      wq2, wkv2, tab, jnp.asarray(sel_k, BF16), jnp.asarray(sel_q, BF16), jnp.asarray(one_k), jnp.asarray(one_q))


def _attn_step(maps, state, j, kind, t):
    m_ref, l_ref, acc_ref = state
    start = pl.multiple_of(j * t, t)

    def logits(mp):
        return jnp.dot(mp["load_k"](start), mp["load_qt"](), preferred_element_type=F32)

    s_all = [logits(mp) for mp in maps]
    for idx, mp in enumerate(maps):
        s = mp["bias"](s_all[idx], kind)
        m_prev = m_ref[idx]
        m_new = jnp.maximum(m_prev, jnp.max(s, axis=0, keepdims=True))
        alpha = jnp.exp2(m_prev - m_new)
        p = jnp.exp2(s - m_new)
        l_ref[idx] = alpha * l_ref[idx] + jnp.sum(p, axis=0, keepdims=True)
        acc_ref[idx] = alpha * acc_ref[idx] + jnp.dot(mp["load_vt"](start), p.astype(BF16),
                                                      preferred_element_type=F32)
        m_ref[idx] = m_new


def _attn_run(maps, state, qi, t, near_tile):
    m_ref, l_ref, acc_ref = state
    m_ref[...] = jnp.full(m_ref.shape, -jnp.inf, F32)
    l_ref[...] = jnp.zeros(l_ref.shape, F32)
    acc_ref[...] = jnp.zeros(acc_ref.shape, F32)

    def body(j, c):
        _attn_step(maps, state, j, "far", t)
        return c

    if near_tile:
        lax.fori_loop(0, jnp.maximum(qi - 1, 0), body, 0)

        @pl.when(qi >= 1)
        def _():
            _attn_step(maps, state, qi - 1, "near", t)
    else:
        lax.fori_loop(0, qi, body, 0)
    _attn_step(maps, state, qi, "diag", t)
    return [acc_ref[idx] / l_ref[idx] for idx in range(len(maps))]


def _causal_bias(s, kind):
    if kind != "diag":
        return s
    key = lax.broadcasted_iota(jnp.int32, s.shape, 0)
    qry = lax.broadcasted_iota(jnp.int32, s.shape, 1)
    return jnp.where(key <= qry, s, -jnp.inf)


def _attn_scratch(n_maps, dv, t):
    return [pltpu.VMEM((n_maps, 1, t), F32), pltpu.VMEM((n_maps, 1, t), F32), pltpu.VMEM((n_maps, dv, t), F32)]


def _diff_attn_kernel(qt_ref, k_ref, vt_ref, bias_ref, lam_ref, g_ref, o_ref, qm_ref, m_ref, l_ref, acc_ref,
                      *, t, lam_init):
    qi = pl.program_id(1)
    row = lax.broadcasted_iota(jnp.int32, (LANES, t), 0)
    lp = lam_ref[...]
    lam = (jnp.exp(jnp.sum(lp[0:1] * lp[1:2], axis=1, keepdims=True))
           - jnp.exp(jnp.sum(lp[2:3] * lp[3:4], axis=1, keepdims=True)) + lam_init)
    maps = []
    for h in range(DIFF_HEADS):
        p, i = divmod(h, 2)
        qt_pair = qt_ref[0, p * LANES:(p + 1) * LANES, :]

        def bias(s, kind, h=h):
            if kind == "far":
                return s
            return bias_ref[h, 0 if kind == "diag" else 1] + s

        load_k = lambda start, p=p: k_ref[0, pl.ds(start, t), p * LANES:(p + 1) * LANES]
        load_vt = lambda start, h=h: vt_ref[0, h * DIFF_V_DIM:(h + 1) * DIFF_V_DIM, pl.ds(start, t)]
        for mp in range(2):
            idx = 2 * h + mp
            lo = (2 * i + mp) * DIFF_QK_DIM
            qm_ref[idx] = jnp.where((row >= lo) & (row < lo + DIFF_QK_DIM), qt_pair, jnp.zeros_like(qt_pair))
            maps.append(dict(load_qt=lambda idx=idx: qm_ref[idx], load_k=load_k, load_vt=load_vt, bias=bias))
    outs = _attn_run(maps, (m_ref, l_ref, acc_ref), qi, t, near_tile=True)
    heads = []
    for h in range(DIFF_HEADS):
        o = outs[2 * h] - lam * outs[2 * h + 1]
        ms = jnp.mean(o * o, axis=0, keepdims=True)
        heads.append(o * lax.rsqrt(ms + NORM_EPS))
    ot = jnp.concatenate(heads, axis=0)
    o_ref[0] = (ot.T * g_ref[...] * (1.0 - lam_init)).astype(o_ref.dtype)


def _diff_attention(qt, k, vt, bias_tiles, diff_lambda, subln_g, lam_init, t):
    b, s, _ = k.shape
    n_maps = 2 * DIFF_HEADS
    g_all = jnp.tile(subln_g, DIFF_HEADS).reshape(1, DIFF_WIDTH)
    return pl.pallas_call(
        functools.partial(_diff_attn_kernel, t=t, lam_init=lam_init),
        grid=(b, s // t),
        in_specs=[
            pl.BlockSpec((1, DIFF_WIDTH, t), lambda bi, qi: (bi, 0, qi)),
            pl.BlockSpec((1, s, DIFF_WIDTH), lambda bi, qi: (bi, 0, 0)),
            pl.BlockSpec((1, DIFF_WIDTH, s), lambda bi, qi: (bi, 0, 0)),
            pl.BlockSpec((DIFF_HEADS, 2, t, t), lambda bi, qi: (0, 0, 0, 0)),
            pl.BlockSpec((4, DIFF_QK_DIM), lambda bi, qi: (0, 0)),
            pl.BlockSpec((1, DIFF_WIDTH), lambda bi, qi: (0, 0)),
        ],
        out_specs=pl.BlockSpec((1, t, DIFF_WIDTH), lambda bi, qi: (bi, qi, 0)),
        out_shape=jax.ShapeDtypeStruct((b, s, DIFF_WIDTH), BF16),
        scratch_shapes=[pltpu.VMEM((n_maps, LANES, t), BF16)] + _attn_scratch(n_maps, DIFF_V_DIM, t),
        compiler_params=_cparams(("parallel", "arbitrary")),
        name="diff_attention",
    )(qt, k, vt, bias_tiles, diff_lambda, g_all)


def _fox_attn_kernel(qt_ref, k_ref, vt_ref, o_ref, qm_ref, m_ref, l_ref, acc_ref, *, t):
    qi = pl.program_id(1)
    row = lax.broadcasted_iota(jnp.int32, (2 * LANES, t), 0)
    maps = []
    for h in range(FOX_HEADS):
        p, i = divmod(h, 2)
        blk = 2 * p * LANES
        qt_pair = qt_ref[0, blk:blk + 2 * LANES, :]
        lo = i * FOX_HEAD_DIM
        glo = LANES + 2 * GATE_PIECES * i
        own = ((row >= lo) & (row < lo + FOX_HEAD_DIM)) | ((row >= glo) & (row < glo + 2 * GATE_PIECES))
        qm_ref[h] = jnp.where(own, qt_pair, jnp.zeros_like(qt_pair))
        load_k = lambda start, blk=blk: k_ref[0, pl.ds(start, t), blk:blk + 2 * LANES]
        load_vt = lambda start, h=h: vt_ref[0, h * FOX_HEAD_DIM:(h + 1) * FOX_HEAD_DIM, pl.ds(start, t)]
        maps.append(dict(load_qt=lambda h=h: qm_ref[h], load_k=load_k, load_vt=load_vt, bias=_causal_bias))
    outs = _attn_run(maps, (m_ref, l_ref, acc_ref), qi, t, near_tile=False)
    o_ref[0] = jnp.concatenate(outs, axis=0).T.astype(o_ref.dtype)


def _fox_attention(qt, k, vt, t):
    b, s, _ = k.shape
    return pl.pallas_call(
        functools.partial(_fox_attn_kernel, t=t),
        grid=(b, s // t),
        in_specs=[
            pl.BlockSpec((1, FOX_QK_PAD, t), lambda bi, qi: (bi, 0, qi)),
            pl.BlockSpec((1, s, FOX_QK_PAD), lambda bi, qi: (bi, 0, 0)),
            pl.BlockSpec((1, FOX_WIDTH, s), lambda bi, qi: (bi, 0, 0)),
        ],
        out_specs=pl.BlockSpec((1, t, FOX_WIDTH), lambda bi, qi: (bi, qi, 0)),
        out_shape=jax.ShapeDtypeStruct((b, s, FOX_WIDTH), BF16),
        scratch_shapes=[pltpu.VMEM((FOX_HEADS, 2 * LANES, t), BF16)] + _attn_scratch(FOX_HEADS, FOX_HEAD_DIM, t),
        compiler_params=_cparams(("parallel", "arbitrary")),
        name="fox_attention",
    )(qt, k, vt)


def _mla_attn_kernel(qt_ref, k_ref, vt_ref, o_ref, m_ref, l_ref, acc_ref, *, t):
    qi = pl.program_id(1)
    maps = []
    for h in range(MLA_HEADS):
        load_qt = lambda h=h: qt_ref[0, h * LANES:(h + 1) * LANES, :]
        load_k = lambda start, h=h: k_ref[0, pl.ds(start, t), h * LANES:(h + 1) * LANES]
        load_vt = lambda start, h=h: vt_ref[0, h * MLA_V_DIM:(h + 1) * MLA_V_DIM, pl.ds(start, t)]
        maps.append(dict(load_qt=load_qt, load_k=load_k, load_vt=load_vt, bias=_causal_bias))
    outs = _attn_run(maps, (m_ref, l_ref, acc_ref), qi, t, near_tile=False)
    o_ref[0] = jnp.concatenate(outs, axis=0).T.astype(o_ref.dtype)


def _mla_attention(qt, k, vt, t):
    b, s, _ = k.shape
    return pl.pallas_call(
        functools.partial(_mla_attn_kernel, t=t),
        grid=(b, s // t),
        in_specs=[
            pl.BlockSpec((1, MLA_QK_PAD, t), lambda bi, qi: (bi, 0, qi)),
            pl.BlockSpec((1, s, MLA_QK_PAD), lambda bi, qi: (bi, 0, 0)),
            pl.BlockSpec((1, MLA_WIDTH, s), lambda bi, qi: (bi, 0, 0)),
        ],
        out_specs=pl.BlockSpec((1, t, MLA_WIDTH), lambda bi, qi: (bi, qi, 0)),
        out_shape=jax.ShapeDtypeStruct((b, s, MLA_WIDTH), BF16),
        scratch_shapes=_attn_scratch(MLA_HEADS, MLA_V_DIM, t),
        compiler_params=_cparams(("parallel", "arbitrary")),
        name="mla_attention",
    )(qt, k, vt)


def _outproj_kernel(*refs, d_model, with_router):
    if with_router:
        (od_ref, of_ref, om_ref, w_ref, x_ref, mod_ref, g_ref, rw_ref,
         xo_ref, h_ref, route_ref, cnt_ref, carry_ref) = refs
    else:
        od_ref, of_ref, om_ref, w_ref, x_ref, mod_ref, g_ref, xo_ref, h_ref = refs
    d = d_model
    mix = jnp.dot(od_ref[0], w_ref[0:DIFF_WIDTH, :], preferred_element_type=F32)
    mix += jnp.dot(of_ref[0], w_ref[DIFF_WIDTH:DIFF_WIDTH + FOX_WIDTH, :], preferred_element_type=F32)
    mix += jnp.dot(om_ref[0], w_ref[DIFF_WIDTH + FOX_WIDTH:, :], preferred_element_type=F32)
    g_a = mod_ref[0, :, 2 * d:3 * d]
    sh_f = mod_ref[0, :, 3 * d:4 * d]
    sc_f = mod_ref[0, :, 4 * d:5 * d]
    xn = x_ref[0] + g_a * mix
    xo_ref[0] = xn
    h = _rms(xn, g_ref[...]) * (1.0 + sc_f) + sh_f
    h_ref[0] = h.astype(h_ref.dtype)
    if with_router:
        ts = h.shape[0]

        @pl.when((pl.program_id(0) == 0) & (pl.program_id(1) == 0))
        def _():
            carry_ref[...] = jnp.zeros_like(carry_ref)

        logits = jnp.dot(h, rw_ref[...], preferred_element_type=F32, precision=lax.Precision.HIGHEST)
        lane = lax.broadcasted_iota(jnp.int32, logits.shape, 1).astype(F32)
        neg = -jnp.inf
        lg = jnp.where(lane < N_EXPERTS, logits, neg)
        m1 = jnp.max(lg, axis=1, keepdims=True)
        i1 = jnp.min(jnp.where(lg == m1, lane, float(LANES)), axis=1, keepdims=True)
        lg2 = jnp.where(lane == i1, neg, lg)
        m2 = jnp.max(lg2, axis=1, keepdims=True)
        i2 = jnp.min(jnp.where(lg2 == m2, lane, float(LANES)), axis=1, keepdims=True)
        e2 = jnp.exp(m2 - m1)
        w1 = 1.0 / (1.0 + e2)
        w2 = e2 / (1.0 + e2)
        sel1 = lane == i1
        sel2 = lane == i2
        mask = jnp.where(sel1 | sel2, 1.0, 0.0)
        row = lax.broadcasted_iota(jnp.int32, (ts, ts), 0)
        col = lax.broadcasted_iota(jnp.int32, (ts, ts), 1)
        before = jnp.where(row > col, 1.0, 0.0).astype(BF16)
        excl = jnp.dot(before, mask.astype(BF16), preferred_element_type=F32) + carry_ref[0:1, :]
        r1 = jnp.sum(jnp.where(sel1, excl, 0.0), axis=1, keepdims=True)
        r2 = jnp.sum(jnp.where(sel2, excl, 0.0), axis=1, keepdims=True)
        total = excl[ts - 1:ts, :] + mask[ts - 1:ts, :]
        carry_ref[...] = jnp.broadcast_to(total, carry_ref.shape)
        cnt_ref[...] = jnp.broadcast_to(total, cnt_ref.shape)
        route = jnp.zeros_like(logits)
        for k, val in enumerate((i1, i2, w1, w2, r1, r2)):
            route = jnp.where(lane == float(k), val, route)
        route_ref[0] = route


def _outproj(od, of, om, w_out_b, x, mod_l, g, router_pad, ts):
    b, s, d = x.shape
    n6 = mod_l.shape[-1]
    with_router = router_pad is not None
    tok = lambda bi, si: (bi, si, 0)
    const2 = lambda bi, si: (0, 0)
    in_specs = [
        pl.BlockSpec((1, ts, DIFF_WIDTH), tok),
        pl.BlockSpec((1, ts, FOX_WIDTH), tok),
        pl.BlockSpec((1, ts, MLA_WIDTH), tok),
        pl.BlockSpec((d, d), const2),
        pl.BlockSpec((1, ts, d), tok),
        pl.BlockSpec((1, 1, n6), lambda bi, si: (bi, 0, 0)),
        pl.BlockSpec((1, d), const2),
    ]
    args = [od, of, om, w_out_b, x, mod_l.reshape(b, 1, n6), g.reshape(1, d)]
    out_specs = [pl.BlockSpec((1, ts, d), tok), pl.BlockSpec((1, ts, d), tok)]
    out_shape = [jax.ShapeDtypeStruct((b, s, d), F32), jax.ShapeDtypeStruct((b, s, d), F32 if with_router else BF16)]
    scratch = []
    if with_router:
        in_specs.append(pl.BlockSpec((d, LANES), const2))
        args.append(router_pad)
        out_specs += [pl.BlockSpec((1, ts, LANES), tok), pl.BlockSpec((8, LANES), const2)]
        out_shape += [jax.ShapeDtypeStruct((b, s, LANES), F32), jax.ShapeDtypeStruct((8, LANES), F32)]
        scratch = [pltpu.VMEM((8, LANES), F32)]
    return pl.pallas_call(
        functools.partial(_outproj_kernel, d_model=d, with_router=with_router),
        grid=(b, s // ts),
        in_specs=in_specs,
        out_specs=out_specs,
        out_shape=out_shape,
        scratch_shapes=scratch,
        compiler_params=_cparams(("arbitrary", "arbitrary") if with_router else ("parallel", "parallel")),
        name="outproj_router" if with_router else "outproj",
    )(*args)


def _finish(x_ref, mod_ref, acc, fg_ref, o_ref, d, final):
    g_f = mod_ref[0, :, 5 * d:6 * d]
    xo = x_ref[0] + g_f * acc
    if final:
        xo = _rms(xo, fg_ref[...])
    o_ref[0] = xo


def _ffn_kernel(h_ref, wg_ref, wu_ref, wd_ref, x_ref, mod_ref, fg_ref, o_ref, acc_ref, *, d_model, final):
    j = pl.program_id(2)

    @pl.when(j == 0)
    def _():
        acc_ref[...] = jnp.zeros_like(acc_ref)

    h = h_ref[0]
    a = jnp.dot(h, wg_ref[...], preferred_element_type=F32)
    u = jnp.dot(h, wu_ref[...], preferred_element_type=F32)
    act = (_silu(a) * u).astype(BF16)
    acc_ref[...] += jnp.dot(act, wd_ref[...], preferred_element_type=F32)

    @pl.when(j == pl.num_programs(2) - 1)
    def _():
        _finish(x_ref, mod_ref, acc_ref[...], fg_ref, o_ref, d_model, final)


def _ffn_dense(h, wg, wu, wd, x, mod_l, final_g, final, tm, tf):
    b, s, d = x.shape
    n6 = mod_l.shape[-1]
    dff = wg.shape[1]
    tok = lambda bi, si, j: (bi, si, 0)
    return pl.pallas_call(
        functools.partial(_ffn_kernel, d_model=d, final=final),
        grid=(b, s // tm, dff // tf),
        in_specs=[
            pl.BlockSpec((1, tm, d), tok),
            pl.BlockSpec((d, tf), lambda bi, si, j: (0, j)),
            pl.BlockSpec((d, tf), lambda bi, si, j: (0, j)),
            pl.BlockSpec((tf, d), lambda bi, si, j: (j, 0)),
            pl.BlockSpec((1, tm, d), tok),
            pl.BlockSpec((1, 1, n6), lambda bi, si, j: (bi, 0, 0)),
            pl.BlockSpec((1, d), lambda bi, si, j: (0, 0)),
        ],
        out_specs=pl.BlockSpec((1, tm, d), tok),
        out_shape=jax.ShapeDtypeStruct((b, s, d), F32),
        scratch_shapes=[pltpu.VMEM((tm, d), F32)],
        compiler_params=_cparams(("parallel", "parallel", "arbitrary")),
        name="ffn_dense",
    )(h, wg, wu, wd, x, mod_l.reshape(b, 1, n6), final_g.reshape(1, d))


def _route_plan(route, counts, n_tok, tile):
    i32 = jnp.int32
    r = route.reshape(n_tok, LANES)
    cnt = counts[0, :N_EXPERTS].astype(i32)
    padded = (cnt + tile - 1) // tile * tile
    ends = jnp.cumsum(padded)
    off = ends - padded
    dest_a = off[r[:, 0].astype(i32)] + r[:, 4].astype(i32)
    dest_b = off[r[:, 1].astype(i32)] + r[:, 5].astype(i32)
    n_tiles = 2 * n_tok // tile + N_EXPERTS
    tile_idx = jnp.arange(n_tiles, dtype=i32)
    te = jnp.minimum(jnp.sum((tile_idx[:, None] * tile >= ends[None, :]).astype(i32), axis=1), N_EXPERTS - 1)
    n_valid = ends[-1] // tile
    te = jnp.where(tile_idx < n_valid, te, te[n_valid - 1])
    fill_start = ends - tile
    has_rows = (padded > 0).astype(i32)
    return dest_a, dest_b, te, n_valid.reshape(1), fill_start, has_rows, n_tiles


def _dispatch_kernel(da_ref, db_ref, fill_ref, has_ref, nv_ref, h_ref, xs_ref, zero_ref, sem, *, tm, tile, n_tiles):
    i = pl.program_id(0)

    @pl.when(i == 0)
    def _():
        zero_ref[...] = jnp.zeros_like(zero_ref)

        def zero_tile(start):
            cp = pltpu.make_async_copy(zero_ref, xs_ref.at[pl.ds(pl.multiple_of(start, tile), tile), :], sem.at[2])
            cp.start()
            cp.wait()

        for e in range(N_EXPERTS):
            @pl.when(has_ref[e] > 0)
            def _():
                zero_tile(fill_ref[e])

            @pl.when(n_tiles - 1 - e >= nv_ref[0])
            def _():
                zero_tile((n_tiles - 1 - e) * tile)

    base = i * tm

    def row_copy(r, slot, dest_ref):
        return pltpu.make_async_copy(h_ref.at[pl.ds(r, 1), :], xs_ref.at[pl.ds(dest_ref[base + r], 1), :],
                                     sem.at[slot])

    def start_rows(r, c):
        row_copy(r, 0, da_ref).start()
        row_copy(r, 1, db_ref).start()
        return c

    def wait_rows(r, c):
        row_copy(r, 0, da_ref).wait()
        row_copy(r, 1, db_ref).wait()
        return c

    lax.fori_loop(0, tm, start_rows, 0, unroll=8)
    lax.fori_loop(0, tm, wait_rows, 0, unroll=8)


def _moe_dispatch(h2d, dest_a, dest_b, fill_start, has_rows, n_valid, n_tiles, tile, tm):
    n_tok, d = h2d.shape
    grid_spec = pltpu.PrefetchScalarGridSpec(
        num_scalar_prefetch=5,
        grid=(n_tok // tm,),
        in_specs=[pl.BlockSpec((tm, d), lambda i, *_: (i, 0))],
        out_specs=pl.BlockSpec(memory_space=pl.ANY),
        scratch_shapes=[pltpu.VMEM((tile, d), F32), pltpu.SemaphoreType.DMA((3,))],
    )
    return pl.pallas_call(
        functools.partial(_dispatch_kernel, tm=tm, tile=tile, n_tiles=n_tiles),
        grid_spec=grid_spec,
        out_shape=jax.ShapeDtypeStruct((n_tiles * tile, d), F32),
        compiler_params=_cparams(("arbitrary",)),
        name="moe_dispatch",
    )(dest_a, dest_b, fill_start, has_rows, n_valid, h2d)


def _experts_kernel(te_ref, nv_ref, xs_ref, wg_ref, wu_ref, wd_ref, ys_ref, xb_ref, acc_ref):
    n = pl.program_id(0)
    j = pl.program_id(1)
    last = pl.num_programs(1) - 1
    valid = n < nv_ref[0]

    @pl.when(valid)
    def _():
        @pl.when(j == 0)
        def _():
            xb_ref[...] = xs_ref[...].astype(BF16)
            acc_ref[...] = jnp.zeros_like(acc_ref)

        xb = xb_ref[...]
        a = jnp.dot(xb, wg_ref[0], preferred_element_type=F32)
        u = jnp.dot(xb, wu_ref[0], preferred_element_type=F32)
        act = (_silu(a) * u).astype(BF16)
        acc_ref[...] += jnp.dot(act, wd_ref[0], preferred_element_type=F32)

        @pl.when(j == last)
        def _():
            ys_ref[...] = acc_ref[...]

    @pl.when(jnp.logical_not(valid) & (j == last))
    def _():
        ys_ref[...] = jnp.zeros_like(ys_ref)


def _moe_experts(xs, te, n_valid, wg, wu, wd, tile, tf):
    rows, d = xs.shape
    ne, _, dff = wg.shape
    nj = dff // tf
    n_tiles = rows // tile

    def jj(n, j, nv):
        return jnp.where(n < nv[0], j, nj - 1)

    grid_spec = pltpu.PrefetchScalarGridSpec(
        num_scalar_prefetch=2,
        grid=(n_tiles, nj),
        in_specs=[
            pl.BlockSpec((tile, d), lambda n, j, te, nv: (jnp.minimum(n, nv[0] - 1), 0)),
            pl.BlockSpec((1, d, tf), lambda n, j, te, nv: (te[n], 0, jj(n, j, nv))),
            pl.BlockSpec((1, d, tf), lambda n, j, te, nv: (te[n], 0, jj(n, j, nv))),
            pl.BlockSpec((1, tf, d), lambda n, j, te, nv: (te[n], jj(n, j, nv), 0)),
        ],
        out_specs=pl.BlockSpec((tile, d), lambda n, j, te, nv: (n, 0)),
        scratch_shapes=[pltpu.VMEM((tile, d), BF16), pltpu.VMEM((tile, d), F32)],
    )
    return pl.pallas_call(
        _experts_kernel,
        grid_spec=grid_spec,
        out_shape=jax.ShapeDtypeStruct((rows, d), F32),
        compiler_params=_cparams(("arbitrary", "arbitrary")),
        name="moe_experts",
    )(te, n_valid, xs, wg, wu, wd)


def _combine_kernel(da_ref, db_ref, ys_ref, x_ref, route_ref, mod_ref, fg_ref, o_ref, ybuf, sem,
                    *, tm, d_model, final):
    i = pl.program_id(0)
    n_steps = pl.num_programs(0)

    def row_copy(step, slot, r, which, dest_ref):
        return pltpu.make_async_copy(ys_ref.at[pl.ds(dest_ref[step * tm + r], 1), :],
                                     ybuf.at[slot, which, pl.ds(r, 1), :], sem.at[slot, which])

    def start_tile(step, slot):
        def body(r, c):
            row_copy(step, slot, r, 0, da_ref).start()
            row_copy(step, slot, r, 1, db_ref).start()
            return c

        lax.fori_loop(0, tm, body, 0, unroll=8)

    @pl.when(i == 0)
    def _():
        start_tile(0, 0)

    @pl.when(i + 1 < n_steps)
    def _():
        start_tile(i + 1, (i + 1) % 2)

    slot = i % 2

    def wait_rows(r, c):
        row_copy(i, slot, r, 0, da_ref).wait()
        row_copy(i, slot, r, 1, db_ref).wait()
        return c

    lax.fori_loop(0, tm, wait_rows, 0, unroll=8)
    route = route_ref[0]
    lane = lax.broadcasted_iota(jnp.int32, route.shape, 1)
    wa = jnp.sum(jnp.where(lane == 2, route, 0.0), axis=1, keepdims=True)
    wb = jnp.sum(jnp.where(lane == 3, route, 0.0), axis=1, keepdims=True)
    y = wa * ybuf[slot, 0] + wb * ybuf[slot, 1]
    _finish(x_ref, mod_ref, y, fg_ref, o_ref, d_model, final)


def _moe_combine(ys, dest_a, dest_b, x, route, mod_l, final_g, final, tm):
    b, s, d = x.shape
    n6 = mod_l.shape[-1]
    spb = s // tm
    tok = lambda i, *_: (i // spb, i % spb, 0)
    grid_spec = pltpu.PrefetchScalarGridSpec(
        num_scalar_prefetch=2,
        grid=(b * spb,),
        in_specs=[
            pl.BlockSpec(memory_space=pl.ANY),
            pl.BlockSpec((1, tm, d), tok),
            pl.BlockSpec((1, tm, LANES), tok),
            pl.BlockSpec((1, 1, n6), lambda i, *_: (i // spb, 0, 0)),
            pl.BlockSpec((1, d), lambda i, *_: (0, 0)),
        ],
        out_specs=pl.BlockSpec((1, tm, d), tok),
        scratch_shapes=[pltpu.VMEM((2, 2, tm, d), F32), pltpu.SemaphoreType.DMA((2, 2))],
    )
    return pl.pallas_call(
        functools.partial(_combine_kernel, tm=tm, d_model=d, final=final),
        grid_spec=grid_spec,
        out_shape=jax.ShapeDtypeStruct((b, s, d), F32),
        compiler_params=_cparams(("arbitrary",)),
        name="moe_combine",
    )(dest_a, dest_b, ys, x, route, mod_l.reshape(b, 1, n6), final_g.reshape(1, d))


def _moe_routed(h2, route, counts, wg, wu, wd, x, mod_l, final_g, final, tile, tf, tm_dispatch, tm_combine):
    b, s, d = x.shape
    n_tok = b * s
    dest_a, dest_b, te, n_valid, fill_start, has_rows, n_tiles = _route_plan(route, counts, n_tok, tile)
    xs = _moe_dispatch(h2.reshape(n_tok, d), dest_a, dest_b, fill_start, has_rows, n_valid, n_tiles, tile,
                       tm_dispatch)
    ys = _moe_experts(xs, te, n_valid, wg, wu, wd, tile, tf)
    return _moe_combine(ys, dest_a, dest_b, x, route, mod_l, final_g, final, tm_combine)


def _prep_w_in(w_in_l):
    d = w_in_l.shape[0]
    sizes = [DIFF_WIDTH, DIFF_WIDTH, DIFF_WIDTH, FOX_WIDTH, FOX_WIDTH, FOX_WIDTH, FOX_HEADS,
             MLA_Q_RANK, MLA_KV_RANK, MLA_ROPE_DIM]
    offs = np.cumsum([0] + sizes)
    sec = [w_in_l[:, offs[i]:offs[i + 1]] for i in range(len(sizes))]
    dq, dk, dv, fq, fk, fv, ff, cq, ckv, kr = sec
    half = MLA_ROPE_DIM // 2
    z = lambda n: jnp.zeros((d, n), w_in_l.dtype)
    kr_sw = jnp.concatenate([kr[:, half:], kr[:, :half]], axis=1)
    pad_r = LANES - MLA_NOPE_DIM - MLA_ROPE_DIM
    cols = [dq, dk, dv, fq, fk, fv, cq, ckv,
            z(MLA_NOPE_DIM), kr, z(pad_r), z(MLA_NOPE_DIM), kr_sw, z(pad_r),
            ff, z(LANES - FOX_HEADS)]
    return jnp.concatenate(cols, axis=1).astype(BF16)


def _prep_w_uq(w_uq_l):
    r = w_uq_l.shape[0]
    w = w_uq_l.reshape(r, MLA_HEADS, MLA_NOPE_DIM + MLA_ROPE_DIM)
    nope, rope = w[..., :MLA_NOPE_DIM], w[..., MLA_NOPE_DIM:]
    half = MLA_ROPE_DIM // 2
    rope_sw = jnp.concatenate([rope[..., half:], rope[..., :half]], axis=-1)
    pad_r = LANES - MLA_NOPE_DIM - MLA_ROPE_DIM
    zr = jnp.zeros((r, MLA_HEADS, pad_r), w.dtype)
    zn = jnp.zeros((r, MLA_HEADS, MLA_NOPE_DIM), w.dtype)
    plain = jnp.concatenate([nope, rope, zr], axis=-1).reshape(r, MLA_QK_PAD)
    swapped = jnp.concatenate([zn, rope_sw, zr], axis=-1).reshape(r, MLA_QK_PAD)
    return jnp.concatenate([plain, swapped], axis=1).astype(BF16)


def _prep_w_ukv(w_ukv_l):
    r = w_ukv_l.shape[0]
    w = w_ukv_l.reshape(r, MLA_HEADS, MLA_NOPE_DIM + MLA_V_DIM)
    k_nope, v = w[..., :MLA_NOPE_DIM], w[..., MLA_NOPE_DIM:]
    zk = jnp.zeros((r, MLA_HEADS, LANES - MLA_NOPE_DIM), w.dtype)
    k_pad = jnp.concatenate([k_nope, zk], axis=-1).reshape(r, MLA_QK_PAD)
    return jnp.concatenate([k_pad, v.reshape(r, MLA_WIDTH)], axis=1).astype(BF16)


def _rope_tables(seq):
    half = MLA_ROPE_DIM // 2
    freqs = ROPE_THETA ** (-jnp.arange(half, dtype=F32) / half)
    ang = jnp.arange(seq, dtype=F32)[:, None] * freqs[None, :]
    cos, sin = jnp.cos(ang), jnp.sin(ang)
    scale = (MLA_NOPE_DIM + MLA_ROPE_DIM) ** -0.5 * LOG2E
    pad_r = LANES - MLA_NOPE_DIM - MLA_ROPE_DIM
    ones = jnp.ones((seq, MLA_NOPE_DIM), F32)
    zn = jnp.zeros((seq, MLA_NOPE_DIM), F32)
    zr = jnp.zeros((seq, pad_r), F32)
    c_q = jnp.concatenate([ones, cos, cos, zr], axis=1) * scale
    s_q = jnp.concatenate([zn, -sin, sin, zr], axis=1) * scale
    c_k = jnp.concatenate([zn, cos, cos, zr], axis=1)
    s_k = jnp.concatenate([zn, -sin, sin, zr], axis=1)
    return jnp.concatenate([c_q, s_q, c_k, s_k], axis=1)


def _lambda_init(layer_idx):
    return 0.8 - 0.6 * math.exp(-0.3 * layer_idx)


def _pick(n, cands):
    for c in cands:
        if n % c == 0:
            return c
    return n


def kernel(x, c, w_ada, b_ada, attn_norm, ffn_norm, w_in, b_forget, diff_lambda, diff_subln, rel_bias, mla_q_norm,
           mla_kv_norm, w_uq, w_ukv, w_out, ffn_w_gate, ffn_w_up, ffn_w_down, router_w, moe_w_gate, moe_w_up,
           moe_w_down, final_norm):
    b, s, d = x.shape
    depth = w_ada.shape[0]
    t = ATTN_T
    ts_a = _pick(s, (512, 256))
    ts_w = _pick(s, (512, 256))
    tm = _pick(s, (512, 256))

    mod = _modulation(c, w_ada, b_ada)
    bias_tiles = _bias_tiles(rel_bias, t)
    tab = _rope_tables(s)

    for l in range(depth):
        mod_l = mod[l]
        win_p = _prep_w_in(w_in[l])
        b_misc = jnp.concatenate([b_forget[l], jnp.zeros((LANES - FOX_HEADS,), F32)]).reshape(1, LANES)
        qtd, kd, vtd, qtf, kf, vtf, qtm, km, vtm = _inproj(
            x, mod_l, attn_norm[l], win_p, b_misc, mla_q_norm[l], mla_kv_norm[l],
            _prep_w_uq(w_uq[l]), _prep_w_ukv(w_ukv[l]), tab, ts_a)
        od = _diff_attention(qtd, kd, vtd, bias_tiles, diff_lambda[l], diff_subln[l], _lambda_init(l), t)
        of = _fox_attention(qtf, kf, vtf, t)
        om = _mla_attention(qtm, km, vtm, t)

        is_moe = l % 2 == 1
        i = l // 2
        final = l == depth - 1
        if is_moe:
            rpad = jnp.concatenate([router_w[i], jnp.zeros((d, LANES - N_EXPERTS), F32)], axis=1)
            x, h2, route, counts = _outproj(od, of, om, w_out[l].astype(BF16), x, mod_l, ffn_norm[l], rpad, ts_w)
            dff = moe_w_gate.shape[-1]
            x = _moe_routed(h2, route, counts, moe_w_gate[i].astype(BF16), moe_w_up[i].astype(BF16),
                            moe_w_down[i].astype(BF16), x, mod_l, final_norm, final,
                            MOE_TILE, _pick(dff, (896, 512, 256)), _pick(s, (512, 256)), _pick(s, (256,)))
        else:
            x, h2 = _outproj(od, of, om, w_out[l].astype(BF16), x, mod_l, ffn_norm[l], None, ts_w)
            dff = ffn_w_gate.shape[-1]
            x = _ffn_dense(h2, ffn_w_gate[i].astype(BF16), ffn_w_up[i].astype(BF16), ffn_w_down[i].astype(BF16),
                           x, mod_l, final_norm, final, tm, _pick(dff, (1408, 512, 256)))
    return x
```
